```python
import math, functools
import jax, jax.numpy as jnp
from jax import lax
import numpy as np

D_MODEL = 1024
BATCH = 16
SEQ = 2048
DEPTH = 1
DEC_BATCH = 128
DEC_SEQ = 1
PAST_LEN = 8192
PAGE_SIZE = 128

MLA_HEADS = 8
D_NOPE = 128
D_ROPE = 64
D_QK = D_NOPE + D_ROPE
D_V = 128
D_LATENT = 512
ROPE_THETA = 10000.0
Q_BLOCK = 128
D_INNER = 2 * D_MODEL
SSM_HEADDIM = 64
SSM_HEADS = D_INNER // SSM_HEADDIM
SSM_GROUPS = 4
D_STATE = 128
CONV_W = 4
CONV_DIM = D_INNER + 2 * SSM_GROUPS * D_STATE
SSD_CHUNK = 128
PEER_HEADS = 8
N_KEYS = 128
N_EXPERTS = N_KEYS * N_KEYS
D_KEY = 256
PEER_TOPK = 16
TOK_BLOCK = 128
N_ADA = 6
EPS = 1e-6
IN_SPLITS = (MLA_HEADS * D_QK, D_LATENT, D_ROPE, D_INNER, CONV_DIM, SSM_HEADS, D_MODEL, D_MODEL)
IN_DIM = sum(IN_SPLITS)
IN_OFFSETS = tuple(int(v) for v in np.cumsum(IN_SPLITS)[:-1])

kernel_name = "hybrid_mla_ssd_peer_adaln_step"


def rms_norm(x, g):
    x32 = x.astype(jnp.float32)
    y = x32 * lax.rsqrt(jnp.mean(x32 * x32, axis=-1, keepdims=True) + EPS)
    return (y * g.astype(jnp.float32)).astype(x.dtype)


def rope(x, pos):
    half = D_ROPE // 2
    inv = ROPE_THETA ** (-jnp.arange(half, dtype=jnp.float32) / half)
    ang = pos.astype(jnp.float32)[:, None] * inv[None, :]
    ang = ang.reshape(ang.shape[:1] + (1,) * (x.ndim - 3) + ang.shape[1:])
    cos = jnp.cos(ang).astype(x.dtype)
    sin = jnp.sin(ang).astype(x.dtype)
    x1, x2 = x[..., :half], x[..., half:]
    return jnp.concatenate([x1 * cos - x2 * sin, x2 * cos + x1 * sin], axis=-1)


def mla_keys(ckv, kpe, w_uk, g_k):
    k_nope = jnp.einsum('...c,chd->...hd', ckv, w_uk)
    k_pe = jnp.broadcast_to(kpe[..., None, :], k_nope.shape[:-1] + (D_ROPE,))
    return rms_norm(jnp.concatenate([k_nope, k_pe], axis=-1), g_k)


def mla_prompt(q, ckv, kpe, w_uk, w_uv, g_k):
    b, s = q.shape[:2]
    k = mla_keys(ckv, kpe, w_uk, g_k)
    qb_len = math.gcd(s, Q_BLOCK)
    nblk = s // qb_len
    q_blocks = jnp.moveaxis(q.reshape(b, nblk, qb_len, MLA_HEADS, D_QK), 1, 0)
    starts = jnp.arange(nblk, dtype=jnp.int32) * qb_len
    key_pos = jnp.arange(s, dtype=jnp.int32)
    scale = D_QK ** -0.5

    def one_block(args):
        qi, s0 = args
        q_pos = s0 + jnp.arange(qb_len, dtype=jnp.int32)
        sc = jnp.einsum('bqhd,bkhd->bhqk', qi, k).astype(jnp.float32) * scale
        sc = jnp.where(key_pos[None, :] <= q_pos[:, None], sc, -jnp.inf)
        p = jax.nn.softmax(sc, axis=-1).astype(ckv.dtype)
        o_lat = jnp.einsum('bhqk,bkc->bqhc', p, ckv)
        return jnp.einsum('bqhc,chv->bqhv', o_lat, w_uv)

    o = lax.map(one_block, (q_blocks, starts))
    return jnp.moveaxis(o, 0, 1).reshape(b, s, MLA_HEADS * D_V)


def mla_sample(q, ckv, kpe, cache_ckv, cache_kpe, page_table, w_uk, w_uv, g_k):
    db, t = q.shape[:2]
    past = page_table.shape[1] * cache_ckv.shape[1]
    key_pos = jnp.arange(past + t, dtype=jnp.int32)
    q_pos = past + jnp.arange(t, dtype=jnp.int32)
    mask = key_pos[None, :] <= q_pos[:, None]
    scale = D_QK ** -0.5

    def one_seq(args):
        qi, ckv_i, kpe_i, pages = args
        ckv_all = jnp.concatenate([cache_ckv[pages].reshape(past, D_LATENT), ckv_i], axis=0)
        kpe_all = jnp.concatenate([cache_kpe[pages].reshape(past, D_ROPE), kpe_i], axis=0)
        k = mla_keys(ckv_all, kpe_all, w_uk, g_k)
        sc = jnp.einsum('qhd,khd->hqk', qi, k).astype(jnp.float32) * scale
        sc = jnp.where(mask[None], sc, -jnp.inf)
        p = jax.nn.softmax(sc, axis=-1).astype(ckv_all.dtype)
        o_lat = jnp.einsum('hqk,kc->qhc', p, ckv_all)
        return jnp.einsum('qhc,chv->qhv', o_lat, w_uv)

    o = lax.map(one_seq, (q, ckv, kpe, page_table))
    return o.reshape(db, t, MLA_HEADS * D_V)


def causal_conv(xbc, conv_state, w, bias):
    t = xbc.shape[1]
    xpad = jnp.concatenate([conv_state, xbc], axis=1)
    y = bias
    for k in range(CONV_W):
        y = y + xpad[:, k:k + t] * w[k]
    return jax.nn.silu(y), xpad[:, -(CONV_W - 1):]


def ssd_scan(x, dt, a, bmat, cmat, h0):
    b, t = x.shape[:2]
    r = SSM_HEADS // SSM_GROUPS
    q = math.gcd(t, SSD_CHUNK)
    nc = t // q

    def chunks(z):
        return jnp.moveaxis(z.reshape((b, nc, q) + z.shape[2:]), 1, 0)

    xs = (chunks(x.astype(jnp.float32).reshape(b, t, SSM_GROUPS, r, SSM_HEADDIM)),
          chunks(dt.reshape(b, t, SSM_GROUPS, r)),
          chunks(bmat.astype(jnp.float32)),
          chunks(cmat.astype(jnp.float32)))
    a_gr = a.reshape(SSM_GROUPS, r)
    causal = jnp.tril(jnp.ones((q, q), dtype=bool))

    def step(h, inp):
        xc, dtc, bc, cc = inp
        acs = jnp.cumsum(dtc * a_gr, axis=1)
        seg = acs[:, :, None] - acs[:, None, :]
        decay = jnp.exp(jnp.where(causal[None, :, :, None, None], seg, -jnp.inf))
        cb = jnp.einsum('bign,bjgn->bijg', cc, bc)
        w = cb[..., None] * decay * dtc[:, None]
        y_diag = jnp.einsum('bijgr,bjgrp->bigrp', w, xc)
        y_off = jnp.einsum('bign,bgrpn->bigrp', cc, h) * jnp.exp(acs)[..., None]
        to_end = jnp.exp(acs[:, -1:] - acs) * dtc
        h_new = jnp.exp(acs[:, -1])[..., None, None] * h + jnp.einsum('bjgn,bjgr,bjgrp->bgrpn', bc, to_end, xc)
        return h_new, y_diag + y_off

    h_init = h0.astype(jnp.float32).reshape(b, SSM_GROUPS, r, SSM_HEADDIM, D_STATE)
    h_fin, ys = lax.scan(step, h_init, xs)
    y = jnp.moveaxis(ys, 0, 1).reshape(b, t, SSM_HEADS, SSM_HEADDIM)
    return y, h_fin.reshape(b, SSM_HEADS, SSM_HEADDIM, D_STATE)


def ssm_branch(z, xbc_raw, dt_raw, conv_state, ssm_state, conv_w, conv_b, dt_bias, a_log, d_skip, g_ssm):
    xbc, new_conv = causal_conv(xbc_raw, conv_state, conv_w, conv_b)
    b, t = xbc.shape[:2]
    xs, bm, cm = jnp.split(xbc, [D_INNER, D_INNER + SSM_GROUPS * D_STATE], axis=-1)
    xs = xs.reshape(b, t, SSM_HEADS, SSM_HEADDIM)
    bm = bm.reshape(b, t, SSM_GROUPS, D_STATE)
    cm = cm.reshape(b, t, SSM_GROUPS, D_STATE)
    dt = jax.nn.softplus(dt_raw.astype(jnp.float32) + dt_bias.astype(jnp.float32))
    a = -jnp.exp(a_log.astype(jnp.float32))
    y, new_ssm = ssd_scan(xs, dt, a, bm, cm, ssm_state)
    y = y + d_skip.astype(jnp.float32)[:, None] * xs.astype(jnp.float32)
    y = y.reshape(b, t, D_INNER).astype(z.dtype)
    y = rms_norm(y * jax.nn.silu(z), g_ssm)
    return y, new_conv, new_ssm.astype(ssm_state.dtype)


def peer_ffn(h, wq, keys1, keys2, u_tab, v_tab):
    shp = h.shape
    tok = h.reshape(-1, D_MODEL)
    n = tok.shape[0]
    tok = jnp.pad(tok, ((0, (-n) % TOK_BLOCK), (0, 0)))
    blocks = tok.reshape(-1, TOK_BLOCK, D_MODEL)
    half = D_KEY // 2

    def one_block(hb):
        qh = (hb @ wq).reshape(TOK_BLOCK, PEER_HEADS, D_KEY)
        s1 = jnp.einsum('thd,hnd->thn', qh[..., :half], keys1).astype(jnp.float32)
        s2 = jnp.einsum('thd,hnd->thn', qh[..., half:], keys2).astype(jnp.float32)
        v1, i1 = lax.top_k(s1, PEER_TOPK)
        v2, i2 = lax.top_k(s2, PEER_TOPK)
        cand = (v1[..., :, None] + v2[..., None, :]).reshape(TOK_BLOCK, PEER_HEADS, PEER_TOPK * PEER_TOPK)
        cid = (i1[..., :, None] * N_KEYS + i2[..., None, :]).reshape(TOK_BLOCK, PEER_HEADS, PEER_TOPK * PEER_TOPK)
        top_s, j = lax.top_k(cand, PEER_TOPK)
        ids = jnp.take_along_axis(cid, j, axis=-1)
        g = jax.nn.softmax(top_s, axis=-1).astype(hb.dtype)
        act = jax.nn.gelu(jnp.einsum('thkd,td->thk', u_tab[ids], hb), approximate=False)
        return jnp.einsum('thk,thkd->td', g * act, v_tab[ids])

    out = lax.map(one_block, blocks).reshape(-1, D_MODEL)[:n]
    return out.reshape(shp)


def trunk_layer(x, c, positions, conv_state, ssm_state, attend, lp):
    b, t = x.shape[:2]
    ada = (c @ lp['w_ada'] + lp['b_ada'])[:, None, :]
    sh1, sc1, gt1, sh2, sc2, gt2 = jnp.split(ada, N_ADA, axis=-1)
    h = rms_norm(x, lp['g_mix']) * (1 + sc1) + sh1
    proj = h @ lp['w_in']
    q_raw, ckv_raw, kpe_raw, z, xbc_raw, dt_raw, gate_a, gate_b = jnp.split(proj, IN_OFFSETS, axis=-1)
    q = q_raw.reshape(b, t, MLA_HEADS, D_QK)
    q = rms_norm(jnp.concatenate([q[..., :D_NOPE], rope(q[..., D_NOPE:], positions)], axis=-1), lp['g_q'])
    ckv = rms_norm(ckv_raw, lp['g_ckv'])
    kpe = rope(kpe_raw, positions)
    o_a = attend(q, ckv, kpe)
    y_b, new_conv, new_ssm = ssm_branch(z, xbc_raw, dt_raw, conv_state, ssm_state, lp['conv_w'], lp['conv_b'],
                                        lp['dt_bias'], lp['a_log'], lp['d_skip'], lp['g_ssm'])
    merged = jax.nn.sigmoid(gate_a) * (o_a @ lp['w_br_a']) + jax.nn.sigmoid(gate_b) * (y_b @ lp['w_br_b'])
    x = x + gt1 * (merged @ lp['w_out'])
    h2 = rms_norm(x, lp['g_ffn']) * (1 + sc2) + sh2
    x = x + gt2 * peer_ffn(h2, lp['peer_wq'], lp['peer_keys1'], lp['peer_keys2'], lp['peer_u'], lp['peer_v'])
    return x, ckv, kpe, new_conv, new_ssm


def setup_inputs(seed: int = 0) -> dict:
    key = jax.random.key(seed)
    ks = iter(jax.random.split(key, 48))

    def nrm(shape, scale):
        return jax.random.normal(next(ks), shape, jnp.float32) * scale

    def gain(shape):
        return 1.0 + nrm(shape, 0.02)

    n_pages = PAST_LEN // PAGE_SIZE
    n_used = DEC_BATCH * n_pages
    n_pool = n_used + n_used // 4
    page_table = jax.random.permutation(next(ks), n_pool)[:n_used].reshape(DEC_BATCH, n_pages).astype(jnp.int32)
    dt0 = jnp.exp(jax.random.uniform(next(ks), (DEPTH, SSM_HEADS), jnp.float32, math.log(1e-3), math.log(1e-1)))
    return {
        'x_prompt': nrm((BATCH, SEQ, D_MODEL), 1.0),
        'x_sample': nrm((DEC_BATCH, DEC_SEQ, D_MODEL), 1.0),
        'c_prompt': nrm((BATCH, D_MODEL), 1.0),
        'c_sample': nrm((DEC_BATCH, D_MODEL), 1.0),
        'cache_ckv': nrm((DEPTH, n_pool, PAGE_SIZE, D_LATENT), 1.0),
        'cache_kpe': nrm((DEPTH, n_pool, PAGE_SIZE, D_ROPE), 1.0),
        'page_table': page_table,
        'state_conv': nrm((DEPTH, DEC_BATCH, CONV_W - 1, CONV_DIM), 1.0),
        'state_ssm': nrm((DEPTH, DEC_BATCH, SSM_HEADS, SSM_HEADDIM, D_STATE), 0.5),
        'w_ada': nrm((DEPTH, D_MODEL, N_ADA * D_MODEL), 0.5 * D_MODEL ** -0.5),
        'b_ada': nrm((DEPTH, N_ADA * D_MODEL), 0.01),
        'g_mix': gain((DEPTH, D_MODEL)),
        'w_in': nrm((DEPTH, D_MODEL, IN_DIM), D_MODEL ** -0.5),
        'g_q': gain((DEPTH, D_QK)),
        'g_ckv': gain((DEPTH, D_LATENT)),
        'w_uk': nrm((DEPTH, D_LATENT, MLA_HEADS, D_NOPE), D_LATENT ** -0.5),
        'w_uv': nrm((DEPTH, D_LATENT, MLA_HEADS, D_V), D_LATENT ** -0.5),
        'g_k': gain((DEPTH, D_QK)),
        'conv_w': nrm((DEPTH, CONV_W, CONV_DIM), CONV_W ** -0.5),
        'conv_b': nrm((DEPTH, CONV_DIM), 0.01),
        'dt_bias': dt0 + jnp.log(-jnp.expm1(-dt0)),
        'a_log': jnp.log(jax.random.uniform(next(ks), (DEPTH, SSM_HEADS), jnp.float32, 1.0, 16.0)),
        'd_skip': 1.0 + nrm((DEPTH, SSM_HEADS), 0.1),
        'g_ssm': gain((DEPTH, D_INNER)),
        'w_br_a': nrm((DEPTH, MLA_HEADS * D_V, D_MODEL), (MLA_HEADS * D_V) ** -0.5),
        'w_br_b': nrm((DEPTH, D_INNER, D_MODEL), D_INNER ** -0.5),
        'w_out': nrm((DEPTH, D_MODEL, D_MODEL), D_MODEL ** -0.5),
        'g_ffn': gain((DEPTH, D_MODEL)),
        'peer_wq': nrm((DEPTH, D_MODEL, PEER_HEADS * D_KEY), D_MODEL ** -0.5),
        'peer_keys1': nrm((DEPTH, PEER_HEADS, N_KEYS, D_KEY // 2), (D_KEY // 2) ** -0.5),
        'peer_keys2': nrm((DEPTH, PEER_HEADS, N_KEYS, D_KEY // 2), (D_KEY // 2) ** -0.5),
        'peer_u': nrm((DEPTH, N_EXPERTS, D_MODEL), D_MODEL ** -0.5),
        'peer_v': nrm((DEPTH, N_EXPERTS, D_MODEL), 0.5),
    }


def reference(x_prompt, x_sample, c_prompt, c_sample, cache_ckv, cache_kpe, page_table, state_conv, state_ssm,
              w_ada, b_ada, g_mix, w_in, g_q, g_ckv, w_uk, w_uv, g_k, conv_w, conv_b, dt_bias, a_log, d_skip,
              g_ssm, w_br_a, w_br_b, w_out, g_ffn, peer_wq, peer_keys1, peer_keys2, peer_u, peer_v):
    past = page_table.shape[1] * cache_ckv.shape[2]
    pos_p = jnp.arange(x_prompt.shape[1], dtype=jnp.int32)
    pos_s = past + jnp.arange(x_sample.shape[1], dtype=jnp.int32)
    xp, xs = x_prompt, x_sample
    ckv_p, kpe_p, conv_p, ssm_p = [], [], [], []
    ckv_s, kpe_s, conv_s, ssm_s = [], [], [], []
    for l in range(DEPTH):
        lp = {'w_ada': w_ada[l], 'b_ada': b_ada[l], 'g_mix': g_mix[l], 'w_in': w_in[l], 'g_q': g_q[l],
              'g_ckv': g_ckv[l], 'conv_w': conv_w[l], 'conv_b': conv_b[l], 'dt_bias': dt_bias[l],
              'a_log': a_log[l], 'd_skip': d_skip[l], 'g_ssm': g_ssm[l], 'w_br_a': w_br_a[l], 'w_br_b': w_br_b[l],
              'w_out': w_out[l], 'g_ffn': g_ffn[l], 'peer_wq': peer_wq[l], 'peer_keys1': peer_keys1[l],
              'peer_keys2': peer_keys2[l], 'peer_u': peer_u[l], 'peer_v': peer_v[l]}
        zero_conv = jnp.zeros((xp.shape[0], CONV_W - 1, CONV_DIM), xp.dtype)
        zero_ssm = jnp.zeros((xp.shape[0], SSM_HEADS, SSM_HEADDIM, D_STATE), state_ssm.dtype)
        attend_p = functools.partial(mla_prompt, w_uk=w_uk[l], w_uv=w_uv[l], g_k=g_k[l])
        xp, a1, a2, a3, a4 = trunk_layer(xp, c_prompt, pos_p, zero_conv, zero_ssm, attend_p, lp)
        ckv_p.append(a1); kpe_p.append(a2); conv_p.append(a3); ssm_p.append(a4)
        attend_s = functools.partial(mla_sample, cache_ckv=cache_ckv[l], cache_kpe=cache_kpe[l],
                                     page_table=page_table, w_uk=w_uk[l], w_uv=w_uv[l], g_k=g_k[l])
        xs, b1, b2, b3, b4 = trunk_layer(xs, c_sample, pos_s, state_conv[l], state_ssm[l], attend_s, lp)
        ckv_s.append(b1); kpe_s.append(b2); conv_s.append(b3); ssm_s.append(b4)
    return (xp, xs, jnp.stack(ckv_p), jnp.stack(kpe_p), jnp.stack(conv_p), jnp.stack(ssm_p),
            jnp.stack(ckv_s), jnp.stack(kpe_s), jnp.stack(conv_s), jnp.stack(ssm_s))
```

```python
import functools
import math

import numpy as np
import jax
import jax.numpy as jnp
from jax import lax
from jax.experimental import pallas as pl
from jax.experimental.pallas import tpu as pltpu

F32 = jnp.float32
BF16 = jnp.bfloat16
I32 = jnp.int32

D_MODEL = 1024
MLA_HEADS = 8
D_NOPE = 128
D_ROPE = 64
D_QK = D_NOPE + D_ROPE
D_QK_PAD = 256
D_V = 128
D_LATENT = 512
ROPE_THETA = 10000.0
PAGE_SIZE = 128
D_INNER = 2 * D_MODEL
SSM_HEADDIM = 64
SSM_HEADS = D_INNER // SSM_HEADDIM
SSM_GROUPS = 4
HEADS_PER_GROUP = SSM_HEADS // SSM_GROUPS
D_STATE = 128
CONV_W = 4
CONV_DIM = D_INNER + 2 * SSM_GROUPS * D_STATE
SSD_CHUNK = 128
PEER_HEADS = 8
N_KEYS = 128
D_KEY = 256
PEER_TOPK = 16
N_ADA = 6
EPS = 1e-6
IN_SPLITS = (MLA_HEADS * D_QK, D_LATENT, D_ROPE, D_INNER, CONV_DIM, SSM_HEADS, D_MODEL, D_MODEL)
IN_OFFSETS = tuple(int(v) for v in np.cumsum(IN_SPLITS)[:-1])

LANES = 128
SUBLANES = 8
VMEM_LIMIT = 56 * 1024 * 1024
NEG_INF = float("-inf")
NT_DIMS = (((1,), (1,)), ((), ()))


def _params(sem):
    return pltpu.CompilerParams(dimension_semantics=sem, vmem_limit_bytes=VMEM_LIMIT)


def _tile(n, pref):
    t = min(n, pref)
    assert n % t == 0, (n, pref)
    return t


def _sigmoid(x):
    return 1.0 / (1.0 + jnp.exp(-x))


def _silu(x):
    return x * _sigmoid(x)


def _softplus(x):
    return jnp.maximum(x, 0.0) + jnp.log1p(jnp.exp(-jnp.abs(x)))


def _rope_tile(pe, cos, sin):
    rot = pltpu.roll(pe, D_ROPE // 2, 1) - pltpu.roll(pe, LANES - D_ROPE // 2, 1)
    return pe * cos + rot * sin


def _ada_kernel(c_ref, w_ref, b_ref, o_ref):
    o_ref[...] = jnp.dot(c_ref[...].astype(BF16), w_ref[...].astype(BF16),
                         preferred_element_type=F32) + b_ref[...]


def _ada(c, w, b):
    r, d = c.shape
    n = w.shape[1]
    tn = _tile(n, 1536)
    return pl.pallas_call(
        _ada_kernel,
        grid=(n // tn,),
        in_specs=[pl.BlockSpec((r, d), lambda j: (0, 0)),
                  pl.BlockSpec((d, tn), lambda j: (0, j)),
                  pl.BlockSpec((1, tn), lambda j: (0, j))],
        out_specs=pl.BlockSpec((r, tn), lambda j: (0, j)),
        out_shape=jax.ShapeDtypeStruct((r, n), F32),
        compiler_params=_params(("arbitrary",)),
        name="ada",
    )(c, w, b)


def _modnorm_kernel(x_ref, g_ref, sc_ref, sh_ref, o_ref):
    x = x_ref[0]
    y = x * lax.rsqrt(jnp.mean(x * x, axis=-1, keepdims=True) + EPS) * g_ref[...]
    o_ref[0] = (y * (1.0 + sc_ref[0]) + sh_ref[0]).astype(o_ref.dtype)


def _mod_spec(arr, ts):
    if arr.shape[1] == 1:
        return pl.BlockSpec((1, 1, arr.shape[2]), lambda b, i: (b, 0, 0))
    return pl.BlockSpec((1, ts, arr.shape[2]), lambda b, i: (b, i, 0))


def _modnorm(x, g, sc, sh):
    gn, s, d = x.shape
    ts = _tile(s, 1024)
    return pl.pallas_call(
        _modnorm_kernel,
        grid=(gn, s // ts),
        in_specs=[pl.BlockSpec((1, ts, d), lambda b, i: (b, i, 0)),
                  pl.BlockSpec((1, d), lambda b, i: (0, 0)),
                  _mod_spec(sc, ts), _mod_spec(sh, ts)],
        out_specs=pl.BlockSpec((1, ts, d), lambda b, i: (b, i, 0)),
        out_shape=jax.ShapeDtypeStruct((gn, s, d), BF16),
        compiler_params=_params(("arbitrary", "arbitrary")),
        name="modnorm",
    )(x, g, sc, sh)


def _qproj_kernel(h_ref, w_ref, cos_ref, sin_ref, gn_ref, gp_ref, o_ref):
    raw = jnp.dot(h_ref[...], w_ref[0], preferred_element_type=F32)
    nope = raw[:, :D_NOPE]
    pe = _rope_tile(raw[:, D_NOPE:], cos_ref[...], sin_ref[...])
    ssq = jnp.sum(nope * nope, axis=-1, keepdims=True) + jnp.sum(pe * pe, axis=-1, keepdims=True)
    r = lax.rsqrt(ssq / D_QK + EPS)
    o_ref[0, 0, :, :D_NOPE] = (nope * r * gn_ref[...]).astype(o_ref.dtype)
    o_ref[0, 0, :, D_NOPE:] = (pe * r * gp_ref[...]).astype(o_ref.dtype)


def _qproj(h, wq, cos, sin, gn, gp, groups):
    t, d = h.shape
    s = t // groups
    tm = _tile(s, 1024)
    nsb = s // tm
    return pl.pallas_call(
        _qproj_kernel,
        grid=(t // tm, MLA_HEADS),
        in_specs=[pl.BlockSpec((tm, d), lambda i, hh: (i, 0)),
                  pl.BlockSpec((1, d, D_QK_PAD), lambda i, hh: (hh, 0, 0)),
                  pl.BlockSpec((tm, LANES), lambda i, hh: (i % nsb, 0)),
                  pl.BlockSpec((tm, LANES), lambda i, hh: (i % nsb, 0)),
                  pl.BlockSpec((1, LANES), lambda i, hh: (0, 0)),
                  pl.BlockSpec((1, LANES), lambda i, hh: (0, 0))],
        out_specs=pl.BlockSpec((1, 1, tm, D_QK_PAD), lambda i, hh: (i // nsb, hh, i % nsb, 0)),
        out_shape=jax.ShapeDtypeStruct((groups, MLA_HEADS, s, D_QK_PAD), BF16),
        compiler_params=_params(("arbitrary", "arbitrary")),
        name="qproj",
    )(h, wq, cos, sin, gn, gp)


def _latent_kernel(h_ref, w_ref, cos_ref, sin_ref, g_ref, ckv_ref, kpe_ref):
    raw = jnp.dot(h_ref[...], w_ref[...], preferred_element_type=F32)
    c = raw[:, :D_LATENT]
    ckv_ref[...] = c * lax.rsqrt(jnp.mean(c * c, axis=-1, keepdims=True) + EPS) * g_ref[...]
    kpe_ref[...] = _rope_tile(raw[:, D_LATENT:], cos_ref[...], sin_ref[...])


def _latent_proj(h, w, cos, sin, g, groups):
    t, d = h.shape
    s = t // groups
    tm = _tile(s, 1024)
    nsb = s // tm
    n = w.shape[1]
    return pl.pallas_call(
        _latent_kernel,
        grid=(t // tm,),
        in_specs=[pl.BlockSpec((tm, d), lambda i: (i, 0)),
                  pl.BlockSpec((d, n), lambda i: (0, 0)),
                  pl.BlockSpec((tm, LANES), lambda i: (i % nsb, 0)),
                  pl.BlockSpec((tm, LANES), lambda i: (i % nsb, 0)),
                  pl.BlockSpec((1, D_LATENT), lambda i: (0, 0))],
        out_specs=[pl.BlockSpec((tm, D_LATENT), lambda i: (i, 0)),
                   pl.BlockSpec((tm, LANES), lambda i: (i, 0))],
        out_shape=[jax.ShapeDtypeStruct((t, D_LATENT), F32),
                   jax.ShapeDtypeStruct((t, LANES), F32)],
        compiler_params=_params(("arbitrary",)),
        name="latent_proj",
    )(h, w, cos, sin, g)


def _mm_kernel(x_ref, w_ref, o_ref):
    o_ref[...] = jnp.dot(x_ref[...], w_ref[...], preferred_element_type=F32).astype(o_ref.dtype)


def _mm(x, w, tm_pref=512, out_dtype=F32, name="mm"):
    t, k = x.shape
    n = w.shape[1]
    tm = _tile(t, tm_pref)
    return pl.pallas_call(
        _mm_kernel,
        grid=(t // tm,),
        in_specs=[pl.BlockSpec((tm, k), lambda i: (i, 0)),
                  pl.BlockSpec((k, n), lambda i: (0, 0))],
        out_specs=pl.BlockSpec((tm, n), lambda i: (i, 0)),
        out_shape=jax.ShapeDtypeStruct((t, n), out_dtype),
        compiler_params=_params(("arbitrary",)),
        name=name,
    )(x, w)


def _keys_kernel(ckv_ref, kpe_ref, wuk_ref, wuv_ref, gn_ref, gp_ref, k_ref, v_ref):
    c = ckv_ref[...].astype(BF16)
    kn = jnp.dot(c, wuk_ref[...], preferred_element_type=F32)
    vv = jnp.dot(c, wuv_ref[...], preferred_element_type=F32)
    pe = kpe_ref[...]
    pss = jnp.sum(pe * pe, axis=-1, keepdims=True)
    for h in range(MLA_HEADS):
        knh = kn[:, h * D_NOPE:(h + 1) * D_NOPE]
        r = lax.rsqrt((jnp.sum(knh * knh, axis=-1, keepdims=True) + pss) / D_QK + EPS)
        k_ref[0, h, :, :D_NOPE] = (knh * r * gn_ref[...]).astype(k_ref.dtype)
        k_ref[0, h, :, D_NOPE:] = (pe * r * gp_ref[...]).astype(k_ref.dtype)
        v_ref[0, h] = vv[:, h * D_V:(h + 1) * D_V].astype(v_ref.dtype)


def _keys(ckv, kpe, wuk, wuv, gn, gp, groups):
    t = ckv.shape[0]
    s = t // groups
    tm = _tile(s, 512)
    nsb = s // tm
    return pl.pallas_call(
        _keys_kernel,
        grid=(groups, nsb),
        in_specs=[pl.BlockSpec((tm, D_LATENT), lambda b, i: (b * nsb + i, 0)),
                  pl.BlockSpec((tm, LANES), lambda b, i: (b * nsb + i, 0)),
                  pl.BlockSpec(wuk.shape, lambda b, i: (0, 0)),
                  pl.BlockSpec(wuv.shape, lambda b, i: (0, 0)),
                  pl.BlockSpec((1, LANES), lambda b, i: (0, 0)),
                  pl.BlockSpec((1, LANES), lambda b, i: (0, 0))],
        out_specs=[pl.BlockSpec((1, MLA_HEADS, tm, D_QK_PAD), lambda b, i: (b, 0, i, 0)),
                   pl.BlockSpec((1, MLA_HEADS, tm, D_V), lambda b, i: (b, 0, i, 0))],
        out_shape=[jax.ShapeDtypeStruct((groups, MLA_HEADS, s, D_QK_PAD), BF16),
                   jax.ShapeDtypeStruct((groups, MLA_HEADS, s, D_V), BF16)],
        compiler_params=_params(("arbitrary", "arbitrary")),
        name="mla_keys",
    )(ckv, kpe, wuk, wuv, gn, gp)


def _flash_kernel(q_ref, k_ref, v_ref, o_ref, *, tq, scale):
    qi = pl.program_id(2)
    q = q_ref[0, 0]

    def step(j, carry, masked):
        m, l, acc = carry
        start = pl.multiple_of(j * tq, tq)
        k = k_ref[0, 0, pl.ds(start, tq), :]
        v = v_ref[0, 0, pl.ds(start, tq), :]
        s = lax.dot_general(q, k, NT_DIMS, preferred_element_type=F32) * scale
        if masked:
            row = lax.broadcasted_iota(I32, (tq, tq), 0)
            col = lax.broadcasted_iota(I32, (tq, tq), 1)
            s = jnp.where(col <= row, s, NEG_INF)
        m_new = jnp.maximum(m, jnp.max(s, axis=-1, keepdims=True))
        alpha = jnp.exp(m - m_new)
        p = jnp.exp(s - m_new)
        l = l * alpha + jnp.sum(p, axis=-1, keepdims=True)
        acc = acc * alpha + jnp.dot(p.astype(BF16), v, preferred_element_type=F32)
        return m_new, l, acc

    init = (jnp.full((tq, 1), NEG_INF, F32), jnp.zeros((tq, 1), F32), jnp.zeros((tq, D_V), F32))
    carry = lax.fori_loop(0, qi, lambda j, c: step(j, c, False), init)
    _, l, acc = step(qi, carry, True)
    o_ref[...] = (acc / l).astype(o_ref.dtype)


def _flash(q, k, v):
    b, hh, s, _ = q.shape
    tq = _tile(s, 512)
    nq = s // tq
    return pl.pallas_call(
        functools.partial(_flash_kernel, tq=tq, scale=D_QK ** -0.5),
        grid=(b, hh, nq),
        in_specs=[pl.BlockSpec((1, 1, tq, D_QK_PAD), lambda bi, h, qi: (bi, h, qi, 0)),
                  pl.BlockSpec((1, 1, s, D_QK_PAD), lambda bi, h, qi: (bi, h, 0, 0)),
                  pl.BlockSpec((1, 1, s, D_V), lambda bi, h, qi: (bi, h, 0, 0))],
        out_specs=pl.BlockSpec((tq, D_V), lambda bi, h, qi: (bi * nq + qi, h)),
        out_shape=jax.ShapeDtypeStruct((b * s, hh * D_V), BF16),
        compiler_params=_params(("arbitrary", "arbitrary", "arbitrary")),
        name="flash_prompt",
    )(q, k, v)


def _qabsorb_kernel(q_ref, wukt_ref, gn_ref, gp_ref, qt_ref, qpe_ref):
    q = q_ref[0, 0].astype(F32)
    qg = (q[:, :D_NOPE] * gn_ref[...]).astype(BF16)
    qt_ref[0] = jnp.dot(qg, wukt_ref[...], preferred_element_type=F32).astype(qt_ref.dtype)
    qpe_ref[0] = (q[:, D_NOPE:] * gp_ref[...]).astype(qpe_ref.dtype)


def _qabsorb(q, wukt, gn, gp):
    db = q.shape[2]
    return pl.pallas_call(
        _qabsorb_kernel,
        grid=(MLA_HEADS,),
        in_specs=[pl.BlockSpec((1, 1, db, D_QK_PAD), lambda h: (0, h, 0, 0)),
                  pl.BlockSpec((D_NOPE, D_LATENT), lambda h: (h, 0)),
                  pl.BlockSpec((1, LANES), lambda h: (0, 0)),
                  pl.BlockSpec((1, LANES), lambda h: (0, 0))],
        out_specs=[pl.BlockSpec((1, db, D_LATENT), lambda h: (h, 0, 0)),
                   pl.BlockSpec((1, db, LANES), lambda h: (h, 0, 0))],
        out_shape=[jax.ShapeDtypeStruct((MLA_HEADS, db, D_LATENT), BF16),
                   jax.ShapeDtypeStruct((MLA_HEADS, db, LANES), BF16)],
        compiler_params=_params(("arbitrary",)),
        name="q_absorb",
    )(q, wukt, gn, gp)


def _page_copies(pt_ref, cache_ckv, cache_kpe, ckv_buf, kpe_buf, sem, chunk, slot, pages_per_chunk):
    copies = []
    for p in range(pages_per_chunk):
        page = pt_ref[chunk * pages_per_chunk + p]
        rows = pl.ds(p * PAGE_SIZE, PAGE_SIZE)
        copies.append(pltpu.make_async_copy(cache_ckv.at[page], ckv_buf.at[slot, rows, :], sem.at[0, slot]))
        copies.append(pltpu.make_async_copy(cache_kpe.at[page], kpe_buf.at[slot, rows, :], sem.at[1, slot]))
    return copies


def _decode_kernel(pt_ref, qt_ref, qpe_ref, q_ref, knew_ref, cnew_ref, wukt_ref, cache_ckv, cache_kpe,
                   o_ref, ckv_buf, kpe_buf, sem, m_sc, l_sc, acc_sc, *, pages_per_chunk, scale):
    s_id = pl.program_id(0)
    c_id = pl.program_id(1)
    nc = pl.num_programs(1)
    step = s_id * nc + c_id
    total = pl.num_programs(0) * nc
    slot = step % 2
    copy_args = (pt_ref, cache_ckv, cache_kpe, ckv_buf, kpe_buf, sem)

    @pl.when(step == 0)
    def _():
        for cp in _page_copies(*copy_args, 0, 0, pages_per_chunk):
            cp.start()

    @pl.when(step + 1 < total)
    def _():
        for cp in _page_copies(*copy_args, step + 1, 1 - slot, pages_per_chunk):
            cp.start()

    @pl.when(c_id == 0)
    def _():
        s_new = jnp.sum(q_ref[0].astype(F32) * knew_ref[0].astype(F32), axis=-1, keepdims=True) * scale
        m_sc[...] = jnp.broadcast_to(s_new, m_sc.shape)
        l_sc[...] = jnp.ones(l_sc.shape, F32)
        acc_sc[...] = jnp.broadcast_to(cnew_ref[0], acc_sc.shape)

    for cp in _page_copies(*copy_args, step, slot, pages_per_chunk):
        cp.wait()

    c32 = ckv_buf[slot]
    cb = c32.astype(BF16)
    tk = cb.shape[0]
    kt = lax.dot_general(wukt_ref[...], cb, NT_DIMS, preferred_element_type=F32)
    hrow = lax.broadcasted_iota(I32, (MLA_HEADS, tk), 0)
    ssq = jnp.zeros((MLA_HEADS, tk), F32)
    for h in range(MLA_HEADS):
        blk = kt[h * D_NOPE:(h + 1) * D_NOPE, :]
        ssq = jnp.where(hrow == h, jnp.sum(blk * blk, axis=0, keepdims=True), ssq)
    kp = kpe_buf[slot]
    kp2 = kp * kp
    kp2_hi = kp2.astype(BF16)
    kp2_lo = (kp2 - kp2_hi.astype(F32)).astype(BF16)
    ones = jnp.ones((MLA_HEADS, D_ROPE), BF16)
    pss = (lax.dot_general(ones, kp2_hi, NT_DIMS, preferred_element_type=F32)
           + lax.dot_general(ones, kp2_lo, NT_DIMS, preferred_element_type=F32))
    spe = lax.dot_general(qpe_ref[0][:, :D_ROPE], kp.astype(BF16), NT_DIMS, preferred_element_type=F32)
    sn = lax.dot_general(qt_ref[0], cb, NT_DIMS, preferred_element_type=F32)
    s = (sn + spe) * lax.rsqrt((ssq + pss) / D_QK + EPS) * scale

    m_old = m_sc[:, :1]
    m_new = jnp.maximum(m_old, jnp.max(s, axis=-1, keepdims=True))
    alpha = jnp.exp(m_old - m_new)
    p = jnp.exp(s - m_new)
    l_new = l_sc[:, :1] * alpha + jnp.sum(p, axis=-1, keepdims=True)
    acc_new = acc_sc[...] * alpha + jnp.dot(p.astype(BF16), cb, preferred_element_type=F32)
    m_sc[...] = jnp.broadcast_to(m_new, m_sc.shape)
    l_sc[...] = jnp.broadcast_to(l_new, l_sc.shape)
    acc_sc[...] = acc_new

    @pl.when(c_id == nc - 1)
    def _():
        o_ref[0] = acc_new / l_new


def _decode_attn(page_table, qt, qpe, q, knew, cnew, wukt, cache_ckv, cache_kpe):
    db, n_pages = page_table.shape
    ppc = _tile(n_pages, 8)
    nc = n_pages // ppc
    tk = ppc * PAGE_SIZE
    grid_spec = pltpu.PrefetchScalarGridSpec(
        num_scalar_prefetch=1,
        grid=(db, nc),
        in_specs=[pl.BlockSpec((1, MLA_HEADS, D_LATENT), lambda s, c, pt: (s, 0, 0)),
                  pl.BlockSpec((1, MLA_HEADS, LANES), lambda s, c, pt: (s, 0, 0)),
                  pl.BlockSpec((1, MLA_HEADS, D_QK_PAD), lambda s, c, pt: (s, 0, 0)),
                  pl.BlockSpec((1, MLA_HEADS, D_QK_PAD), lambda s, c, pt: (s, 0, 0)),
                  pl.BlockSpec((1, 1, D_LATENT), lambda s, c, pt: (s, 0, 0)),
                  pl.BlockSpec(wukt.shape, lambda s, c, pt: (0, 0)),
                  pl.BlockSpec(memory_space=pl.ANY),
                  pl.BlockSpec(memory_space=pl.ANY)],
        out_specs=pl.BlockSpec((1, MLA_HEADS, D_LATENT), lambda s, c, pt: (s, 0, 0)),
        scratch_shapes=[pltpu.VMEM((2, tk, D_LATENT), F32),
                        pltpu.VMEM((2, tk, D_ROPE), F32),
                        pltpu.SemaphoreType.DMA((2, 2)),
                        pltpu.VMEM((MLA_HEADS, LANES), F32),
                        pltpu.VMEM((MLA_HEADS, LANES), F32),
                        pltpu.VMEM((MLA_HEADS, D_LATENT), F32)],
    )
    return pl.pallas_call(
        functools.partial(_decode_kernel, pages_per_chunk=ppc, scale=D_QK ** -0.5),
        grid_spec=grid_spec,
        out_shape=jax.ShapeDtypeStruct((db, MLA_HEADS, D_LATENT), F32),
        compiler_params=_params(("arbitrary", "arbitrary")),
        name="decode_attn",
    )(page_table.reshape(-1), qt, qpe, q, knew, cnew, wukt, cache_ckv, cache_kpe)


def _headmm_kernel(x_ref, w_ref, o_ref):
    o_ref[...] = jnp.dot(x_ref[0].astype(BF16), w_ref[...], preferred_element_type=F32).astype(o_ref.dtype)


def _head_out(olat, wuv):
    hh, db, _ = olat.shape
    return pl.pallas_call(
        _headmm_kernel,
        grid=(hh,),
        in_specs=[pl.BlockSpec((1, db, D_LATENT), lambda h: (h, 0, 0)),
                  pl.BlockSpec((D_LATENT, D_V), lambda h: (0, h))],
        out_specs=pl.BlockSpec((db, D_V), lambda h: (0, h)),
        out_shape=jax.ShapeDtypeStruct((db, hh * D_V), BF16),
        compiler_params=_params(("arbitrary",)),
        name="head_out",
    )(olat, wuv)


def _ssd_kernel(xbc_ref, dt_ref, cw_ref, cb_ref, dtb_ref, alog_ref, dsk_ref, y_ref, ssm_ref,
                carry_ref, ht_ref):
    c_id = pl.program_id(1)
    q = SSD_CHUNK

    @pl.when(c_id == 0)
    def _():
        carry_ref[...] = jnp.zeros(carry_ref.shape, F32)
        ht_ref[...] = jnp.zeros(ht_ref.shape, F32)

    cur = xbc_ref[...]
    ext = jnp.concatenate([carry_ref[...], cur], axis=0)
    acc = cb_ref[...] + cur * cw_ref[CONV_W - 1:CONV_W, :]
    for sft in range(1, CONV_W):
        acc = acc + pltpu.roll(ext, sft, 0)[SUBLANES:, :] * cw_ref[CONV_W - 1 - sft:CONV_W - sft, :]
    xbc = _silu(acc)
    carry_ref[...] = cur[q - SUBLANES:, :]

    dtv = _softplus(dt_ref[...] + dtb_ref[...])
    a = -jnp.exp(alog_ref[...])
    row = lax.broadcasted_iota(I32, (q, LANES), 0)
    acs = dtv * a
    sft = 1
    while sft < q:
        acs = acs + jnp.where(row >= sft, pltpu.roll(acs, sft, 0), 0.0)
        sft *= 2
    acs_t = acs.T
    dt_t = dtv.T
    acs_last = acs[q - 1:q, :]
    to_end = jnp.exp(acs_last - acs) * dtv
    e_acs = jnp.exp(acs)
    e_last = jnp.exp(acs_last)

    col = lax.broadcasted_iota(I32, (q, q), 1)
    causal = lax.broadcasted_iota(I32, (q, q), 0) >= col
    left = col < SSM_HEADDIM
    left_row = left[:1, :]
    b_off = D_INNER
    c_off = D_INNER + SSM_GROUPS * D_STATE
    for g in range(SSM_GROUPS):
        bg = xbc[:, b_off + g * D_STATE:b_off + (g + 1) * D_STATE]
        cg = xbc[:, c_off + g * D_STATE:c_off + (g + 1) * D_STATE].astype(BF16)
        cbm = lax.dot_general(cg, bg.astype(BF16), NT_DIMS, preferred_element_type=F32)
        htg = ht_ref[g]
        y_off = jnp.dot(cg, htg.astype(BF16), preferred_element_type=F32)
        xte = []
        decay = []
        for pr in range(HEADS_PER_GROUP // 2):
            h0 = g * HEADS_PER_GROUP + 2 * pr
            lanes = slice(h0 * SSM_HEADDIM, (h0 + 2) * SSM_HEADDIM)
            xp = xbc[:, lanes]
            xpb = xp.astype(BF16)
            yd = []
            for h in (h0, h0 + 1):
                seg = acs[:, h:h + 1] - acs_t[h:h + 1, :]
                w = cbm * jnp.exp(jnp.where(causal, seg, NEG_INF)) * dt_t[h:h + 1, :]
                yd.append(jnp.dot(w.astype(BF16), xpb, preferred_element_type=F32))
            e_pair = jnp.where(left, e_acs[:, h0:h0 + 1], e_acs[:, h0 + 1:h0 + 2])
            y_pair = (jnp.where(left, yd[0], yd[1])
                      + y_off[:, pr * LANES:(pr + 1) * LANES] * e_pair
                      + dsk_ref[:, lanes] * xp)
            y_ref[:, lanes] = y_pair
            te_pair = jnp.where(left, to_end[:, h0:h0 + 1], to_end[:, h0 + 1:h0 + 2])
            xte.append((xp * te_pair).astype(BF16))
            decay.append(jnp.where(left_row, e_last[:, h0:h0 + 1], e_last[:, h0 + 1:h0 + 2]))
        xte = jnp.concatenate(xte, axis=1)
        decay = jnp.concatenate(decay, axis=1)
        ht_ref[g] = htg * decay + jnp.dot(bg.T.astype(BF16), xte, preferred_element_type=F32)

    @pl.when(c_id == pl.num_programs(1) - 1)
    def _():
        for g in range(SSM_GROUPS):
            ssm_ref[0, g * HEADS_PER_GROUP:(g + 1) * HEADS_PER_GROUP] = ht_ref[g].T.reshape(
                HEADS_PER_GROUP, SSM_HEADDIM, D_STATE)


def _ssd_prompt(xbc_raw, dt_raw, cw, cb, dtb, alog, dsk, batch):
    t = xbc_raw.shape[0]
    s = t // batch
    assert s % SSD_CHUNK == 0
    nc = s // SSD_CHUNK
    full = lambda arr: pl.BlockSpec(arr.shape, lambda b, c: (0, 0))
    return pl.pallas_call(
        _ssd_kernel,
        grid=(batch, nc),
        in_specs=[pl.BlockSpec((SSD_CHUNK, CONV_DIM), lambda b, c: (b * nc + c, 0)),
                  pl.BlockSpec((SSD_CHUNK, LANES), lambda b, c: (b * nc + c, 0)),
                  full(cw), full(cb), full(dtb), full(alog), full(dsk)],
        out_specs=[pl.BlockSpec((SSD_CHUNK, D_INNER), lambda b, c: (b * nc + c, 0)),
                   pl.BlockSpec((1, SSM_HEADS, SSM_HEADDIM, D_STATE), lambda b, c: (b, 0, 0, 0))],
        out_shape=[jax.ShapeDtypeStruct((t, D_INNER), F32),
                   jax.ShapeDtypeStruct((batch, SSM_HEADS, SSM_HEADDIM, D_STATE), F32)],
        scratch_shapes=[pltpu.VMEM((SUBLANES, CONV_DIM), F32),
                        pltpu.VMEM((SSM_GROUPS, D_STATE, HEADS_PER_GROUP * SSM_HEADDIM), F32)],
        compiler_params=_params(("arbitrary", "arbitrary")),
        name="ssd_prompt",
    )(xbc_raw, dt_raw, cw, cb, dtb, alog, dsk)


def _ssm_step_kernel(xr_ref, sx_ref, cwx_ref, cbx_ref, bcr_ref, sbc_ref, cwbc_ref, cbbc_ref,
                     dt_ref, dtb_ref, alog_ref, dsk_ref, st_ref, y_ref, nst_ref):
    xt = cbx_ref[...] + cwx_ref[CONV_W - 1] * xr_ref[0]
    bc = cbbc_ref[...] + cwbc_ref[CONV_W - 1:CONV_W, :] * bcr_ref[0]
    for k in range(CONV_W - 1):
        xt = xt + cwx_ref[k] * sx_ref[0, k]
        bc = bc + cwbc_ref[k:k + 1, :] * sbc_ref[0, k:k + 1, :]
    xt = _silu(xt)
    bc = _silu(bc)
    dtv = _softplus(dt_ref[0] + dtb_ref[...])[:, :SSM_HEADS]
    d_a = jnp.exp(dtv * (-jnp.exp(alog_ref[...]))[:, :SSM_HEADS])
    xdt = xt * dtv
    d_a = jnp.broadcast_to(d_a, xt.shape)
    lane = lax.broadcasted_iota(I32, xt.shape, 1)
    y_t = jnp.zeros(xt.shape, F32)
    for h in range(SSM_HEADS):
        g = h // HEADS_PER_GROUP
        bg = bc[:, g * D_STATE:(g + 1) * D_STATE]
        cg = bc[:, (SSM_GROUPS + g) * D_STATE:(SSM_GROUPS + g + 1) * D_STATE]
        hn = d_a[:, h:h + 1] * st_ref[0, h] + xdt[:, h:h + 1] * bg
        nst_ref[0, h] = hn
        y_t = jnp.where(lane == h, jnp.sum(hn * cg, axis=-1, keepdims=True), y_t)
    y_ref[0] = y_t + dsk_ref[...][:, :SSM_HEADS] * xt


def _ssm_step(xr_t, sx_t, cwx_t, cbx_t, bcr, sbc, cwbc, cbbc, dt_raw, dtb, alog, dsk, state):
    db = state.shape[0]
    p, hh = SSM_HEADDIM, SSM_HEADS
    nbc = 2 * SSM_GROUPS * D_STATE
    c2 = lambda arr: pl.BlockSpec(arr.shape, lambda s: (0,) * arr.ndim)
    return pl.pallas_call(
        _ssm_step_kernel,
        grid=(db,),
        in_specs=[pl.BlockSpec((1, p, hh), lambda s: (s, 0, 0)),
                  pl.BlockSpec((1, CONV_W - 1, p, hh), lambda s: (s, 0, 0, 0)),
                  c2(cwx_t), c2(cbx_t),
                  pl.BlockSpec((1, 1, nbc), lambda s: (s, 0, 0)),
                  pl.BlockSpec((1, CONV_W - 1, nbc), lambda s: (s, 0, 0)),
                  c2(cwbc), c2(cbbc),
                  pl.BlockSpec((1, 1, LANES), lambda s: (s, 0, 0)),
                  c2(dtb), c2(alog), c2(dsk),
                  pl.BlockSpec((1, hh, p, D_STATE), lambda s: (s, 0, 0, 0))],
        out_specs=[pl.BlockSpec((1, p, hh), lambda s: (s, 0, 0)),
                   pl.BlockSpec((1, hh, p, D_STATE), lambda s: (s, 0, 0, 0))],
        out_shape=[jax.ShapeDtypeStruct((db, p, hh), F32),
                   jax.ShapeDtypeStruct(state.shape, F32)],
        compiler_params=_params(("arbitrary",)),
        name="ssm_step",
    )(xr_t, sx_t, cwx_t, cbx_t, bcr, sbc, cwbc, cbbc, dt_raw, dtb, alog, dsk, state)


def _merge_kernel(y_ref, z_ref, oa_ref, gab_a_ref, gab_b_ref, x_ref, gt1_ref, sc2_ref, sh2_ref,
                  gssm_ref, gffn_ref, wa_ref, wb_ref, wo_ref, x1_ref, h2_ref):
    yz = y_ref[...] * _silu(z_ref[...])
    yb = yz * lax.rsqrt(jnp.mean(yz * yz, axis=-1, keepdims=True) + EPS) * gssm_ref[...]
    ma = jnp.dot(oa_ref[...], wa_ref[...], preferred_element_type=F32)
    mb = jnp.dot(yb.astype(BF16), wb_ref[...], preferred_element_type=F32)
    merged = _sigmoid(gab_a_ref[...]) * ma + _sigmoid(gab_b_ref[...]) * mb
    x1 = x_ref[0] + gt1_ref[0] * jnp.dot(merged.astype(BF16), wo_ref[...], preferred_element_type=F32)
    x1_ref[0] = x1
    hn = x1 * lax.rsqrt(jnp.mean(x1 * x1, axis=-1, keepdims=True) + EPS) * gffn_ref[...]
    h2_ref[0] = (hn * (1.0 + sc2_ref[0]) + sh2_ref[0]).astype(h2_ref.dtype)


def _merge(y, z, oa, gates, x, gt1, sc2, sh2, gssm, gffn, wa, wb, wo):
    gn, s, d = x.shape
    ts = _tile(s, 256)
    nsb = s // ts
    tok = lambda w: pl.BlockSpec((ts, w), lambda b, i: (b * nsb + i, 0))
    c2 = lambda arr: pl.BlockSpec(arr.shape, lambda b, i: (0, 0))
    return pl.pallas_call(
        _merge_kernel,
        grid=(gn, nsb),
        in_specs=[tok(D_INNER), tok(D_INNER), tok(d),
                  pl.BlockSpec((ts, d), lambda b, i: (b * nsb + i, 0)),
                  pl.BlockSpec((ts, d), lambda b, i: (b * nsb + i, 1)),
                  pl.BlockSpec((1, ts, d), lambda b, i: (b, i, 0)),
                  _mod_spec(gt1, ts), _mod_spec(sc2, ts), _mod_spec(sh2, ts),
                  c2(gssm), c2(gffn), c2(wa), c2(wb), c2(wo)],
        out_specs=[pl.BlockSpec((1, ts, d), lambda b, i: (b, i, 0)),
                   pl.BlockSpec((1, ts, d), lambda b, i: (b, i, 0))],
        out_shape=[jax.ShapeDtypeStruct((gn, s, d), F32),
                   jax.ShapeDtypeStruct((gn, s, d), BF16)],
        compiler_params=_params(("arbitrary", "arbitrary")),
        name="merge",
    )(y, z, oa, gates, gates, x, gt1, sc2, sh2, gssm, gffn, wa, wb, wo)


N_CAND_ROWS = 80


def _cand_tables():
    flat = np.zeros((N_CAND_ROWS,), np.int32)
    valid = np.zeros((N_CAND_ROWS,), bool)
    for r in range(8):
        flat[r], valid[r] = r, True
        flat[8 + r], valid[8 + r] = 8 + r, True
        flat[16 + r], valid[16 + r] = 16 + r, True
        for ra in range(2, 8):
            flat[8 * (ra + 1) + r] = ra * 16 + r
            valid[8 * (ra + 1) + r] = r < 16 // (ra + 1)
        flat[72 + r], valid[72 + r] = (8 + r) * 16, True
    return flat, valid


def _top16_rows(s):
    n, tm = s.shape
    iota = lax.broadcasted_iota(I32, (n, tm), 0).astype(F32)
    r16 = lax.broadcasted_iota(I32, (PEER_TOPK, tm), 0)
    rank = jnp.full((n, tm), 255.0, F32)
    vals = jnp.zeros((PEER_TOPK, tm), F32)
    for r in range(PEER_TOPK):
        m = jnp.max(s, axis=0, keepdims=True)
        idx = jnp.min(jnp.where(s == m, iota, float(n)), axis=0, keepdims=True)
        hit = iota == idx
        rank = jnp.where(hit, float(r), rank)
        s = jnp.where(hit, NEG_INF, s)
        vals = jnp.where(r16 == r, m, vals)
    return vals, rank


def _peer_select_kernel(h2_ref, wq_ref, k1_ref, k2_ref, flat_ref, valid_ref,
                        e1_ref, n_ref, e2_ref, r2_ref):
    qh = jnp.dot(h2_ref[...], wq_ref[...], preferred_element_type=F32)
    tm = qh.shape[0]
    half = D_KEY // 2
    flat = flat_ref[...]
    valid = valid_ref[...] > 0
    row8 = lax.broadcasted_iota(I32, (SUBLANES, tm), 0)
    for h in range(PEER_HEADS):
        q1 = qh[:, h * D_KEY:h * D_KEY + half].astype(BF16)
        q2 = qh[:, h * D_KEY + half:(h + 1) * D_KEY].astype(BF16)
        s1 = lax.dot_general(k1_ref[h], q1, NT_DIMS, preferred_element_type=F32)
        s2 = lax.dot_general(k2_ref[h], q2, NT_DIMS, preferred_element_type=F32)
        v1, rank1 = _top16_rows(s1)
        v2, rank2 = _top16_rows(s2)
        v2_lo = v2[:SUBLANES]
        groups = [v1[0:1] + v2_lo, v1[0:1] + v2[SUBLANES:], v1[1:2] + v2_lo]
        groups += [v1[ra:ra + 1] + v2_lo for ra in range(2, 8)]
        groups.append(v1[SUBLANES:] + v2[0:1])
        cand = jnp.where(valid, jnp.concatenate(groups, axis=0), NEG_INF)
        work = cand
        self32 = jnp.zeros(cand.shape, F32)
        for _ in range(PEER_TOPK):
            m = jnp.max(work, axis=0, keepdims=True)
            pick = jnp.min(jnp.where(work == m, flat, 1e6), axis=0, keepdims=True)
            hit = flat == pick
            self32 = jnp.where(hit, 1.0, self32)
            work = jnp.where(hit, NEG_INF, work)
        top = v1[0:1] + v2[0:1]
        z = jnp.sum(jnp.where(self32 > 0.0, jnp.exp(cand - top), 0.0), axis=0, keepdims=True)
        cnt_lo = jnp.zeros((SUBLANES, tm), F32)
        per_group = [self32[0:8] + self32[8:16]] + [self32[8 * gidx:8 * gidx + 8] for gidx in range(2, 9)]
        for ra, blk in enumerate(per_group):
            cnt_lo = jnp.where(row8 == ra, jnp.sum(blk, axis=0, keepdims=True), cnt_lo)
        cnt = jnp.concatenate([cnt_lo, self32[72:80]], axis=0)
        n_key = jnp.zeros(s1.shape, F32)
        for ra in range(PEER_TOPK):
            n_key = jnp.where(rank1 == float(ra), cnt[ra:ra + 1], n_key)
        e1_ref[h] = jnp.where(rank1 < PEER_TOPK, jnp.exp(s1 - v1[0:1]), 0.0) / z
        n_ref[h] = n_key
        e2_ref[h] = jnp.where(rank2 < PEER_TOPK, jnp.exp(s2 - v2[0:1]), 0.0)
        r2_ref[h] = rank2


def _peer_select(h2, wq, k1, k2):
    t, d = h2.shape
    tm = _tile(t, 256)
    flat, valid = _cand_tables()
    flat_t = jnp.asarray(np.broadcast_to(flat[:, None].astype(np.float32), (N_CAND_ROWS, tm)).copy())
    valid_t = jnp.asarray(np.broadcast_to(valid[:, None].astype(np.int32), (N_CAND_ROWS, tm)).copy())
    c2 = lambda arr: pl.BlockSpec(arr.shape, lambda i: (0,) * arr.ndim)
    tab = pl.BlockSpec((PEER_HEADS, N_KEYS, tm), lambda i: (0, 0, i))
    shp = jax.ShapeDtypeStruct((PEER_HEADS, N_KEYS, t), F32)
    return pl.pallas_call(
        _peer_select_kernel,
        grid=(t // tm,),
        in_specs=[pl.BlockSpec((tm, d), lambda i: (i, 0)), c2(wq), c2(k1), c2(k2), c2(flat_t), c2(valid_t)],
        out_specs=[tab, tab, tab, tab],
        out_shape=[shp, shp, shp, shp],
        compiler_params=_params(("arbitrary",)),
        name="peer_select",
    )(h2, wq, k1, k2, flat_t, valid_t)


A_PER_BLOCK = 8
E_PER_BLOCK = A_PER_BLOCK * N_KEYS


def _peer_mix_kernel(h2_ref, u_ref, vt_ref, e1_ref, n_ref, e2_ref, r2_ref, x1_ref, gt2_ref,
                     o_ref, acc_ref, w_ref):
    j = pl.program_id(2)

    @pl.when(j == 0)
    def _():
        acc_ref[...] = jnp.zeros(acc_ref.shape, F32)

    st = lax.dot_general(u_ref[...], h2_ref[...], NT_DIMS, preferred_element_type=F32)
    for al in range(A_PER_BLOCK):
        sa = st[al * N_KEYS:(al + 1) * N_KEYS, :]
        gate = jnp.zeros(sa.shape, F32)
        for h in range(PEER_HEADS):
            n_row = n_ref[h, al:al + 1, :]
            e1_row = e1_ref[h, al:al + 1, :]
            gate = gate + e1_row * jnp.where(r2_ref[h] < n_row, e2_ref[h], 0.0)
        act = 0.5 * sa * (1.0 + lax.erf(sa * (2.0 ** -0.5)))
        w_ref[al * N_KEYS:(al + 1) * N_KEYS, :] = (gate * act).astype(w_ref.dtype)
    acc_ref[...] += jnp.dot(vt_ref[...], w_ref[...], preferred_element_type=F32)

    @pl.when(j == pl.num_programs(2) - 1)
    def _():
        o_ref[0] = x1_ref[0] + gt2_ref[0] * acc_ref[...].T


def _peer_mix(h2, u, vt, e1, n, e2, r2, x1, gt2):
    gn, s, d = x1.shape
    ts = _tile(s, 512)
    nsb = s // ts
    ne = u.shape[0] // E_PER_BLOCK
    rows = pl.BlockSpec((PEER_HEADS, A_PER_BLOCK, ts), lambda b, i, j: (0, j, b * nsb + i))
    full = pl.BlockSpec((PEER_HEADS, N_KEYS, ts), lambda b, i, j: (0, 0, b * nsb + i))

    def gt_spec(arr):
        if arr.shape[1] == 1:
            return pl.BlockSpec((1, 1, d), lambda b, i, j: (b, 0, 0))
        return pl.BlockSpec((1, ts, d), lambda b, i, j: (b, i, 0))

    return pl.pallas_call(
        _peer_mix_kernel,
        grid=(gn, nsb, ne),
        in_specs=[pl.BlockSpec((ts, d), lambda b, i, j: (b * nsb + i, 0)),
                  pl.BlockSpec((E_PER_BLOCK, d), lambda b, i, j: (j, 0)),
                  pl.BlockSpec((d, E_PER_BLOCK), lambda b, i, j: (0, j)),
                  rows, rows, full, full,
                  pl.BlockSpec((1, ts, d), lambda b, i, j: (b, i, 0)),
                  gt_spec(gt2)],
        out_specs=pl.BlockSpec((1, ts, d), lambda b, i, j: (b, i, 0)),
        out_shape=jax.ShapeDtypeStruct((gn, s, d), F32),
        scratch_shapes=[pltpu.VMEM((d, ts), F32), pltpu.VMEM((E_PER_BLOCK, ts), BF16)],
        compiler_params=_params(("arbitrary", "arbitrary", "arbitrary")),
        name="peer_mix",
    )(h2, u, vt, e1, n, e2, r2, x1, gt2)


def _rope_tables(positions):
    half = D_ROPE // 2
    inv = ROPE_THETA ** (-jnp.arange(half, dtype=F32) / half)
    ang = positions.astype(F32)[:, None] * inv[None, :]
    pad = jnp.zeros((positions.shape[0], LANES - D_ROPE), F32)
    cos = jnp.cos(ang)
    sin = jnp.sin(ang)
    return jnp.concatenate([cos, cos, pad], axis=1), jnp.concatenate([sin, sin, pad], axis=1)


def _pad_lanes(v, width=LANES):
    v = v.reshape(1, -1)
    return jnp.pad(v, ((0, 0), (0, width - v.shape[1])))


def _prep_weights(w_in, g_q, g_ckv, w_uk, w_uv, g_k, conv_w, conv_b, dt_bias, a_log, d_skip, g_ssm,
                  w_br_a, w_br_b, w_out, g_ffn, peer_wq, peer_keys1, peer_keys2, peer_u, peer_v):
    d = D_MODEL
    o = IN_OFFSETS
    wq = w_in[:, :o[0]].reshape(d, MLA_HEADS, D_QK)
    wq = jnp.pad(wq, ((0, 0), (0, 0), (0, D_QK_PAD - D_QK))).transpose(1, 0, 2).astype(BF16)
    w_lat = jnp.concatenate([w_in[:, o[0]:o[2]], jnp.zeros((d, LANES - D_ROPE), F32)], axis=1).astype(BF16)
    w_dt = jnp.pad(w_in[:, o[4]:o[5]], ((0, 0), (0, LANES - SSM_HEADS))).astype(BF16)
    wuk2 = w_uk.reshape(D_LATENT, MLA_HEADS * D_NOPE)
    return dict(
        wq=wq, w_lat=w_lat,
        w_z=w_in[:, o[2]:o[3]].astype(BF16),
        w_xbc=w_in[:, o[3]:o[4]].astype(BF16),
        w_dt=w_dt,
        w_gates=w_in[:, o[5]:].astype(BF16),
        gq_n=g_q[:D_NOPE].reshape(1, -1), gq_p=_pad_lanes(g_q[D_NOPE:]),
        gk_n=g_k[:D_NOPE].reshape(1, -1), gk_p=_pad_lanes(g_k[D_NOPE:]),
        g_ckv=g_ckv.reshape(1, -1),
        wuk=wuk2.astype(BF16), wukt=wuk2.T.astype(BF16),
        wuv=w_uv.reshape(D_LATENT, MLA_HEADS * D_V).astype(BF16),
        conv_w=conv_w, conv_b=conv_b.reshape(1, -1),
        dt_bias=_pad_lanes(dt_bias), a_log=_pad_lanes(a_log), d_skip=_pad_lanes(d_skip),
        d_skip_x=jnp.repeat(d_skip, SSM_HEADDIM).reshape(1, -1),
        g_ssm=g_ssm.reshape(1, -1), g_ffn=g_ffn.reshape(1, -1),
        w_br_a=w_br_a.astype(BF16), w_br_b=w_br_b.astype(BF16), w_out=w_out.astype(BF16),
        peer_wq=peer_wq.astype(BF16), k1=peer_keys1.astype(BF16), k2=peer_keys2.astype(BF16),
        u=peer_u.astype(BF16), vt=peer_v.T.astype(BF16),
    )


def _trunk(x, ada, positions, wp, g_mix, attend, ssm):
    gn, s, d = x.shape
    t = gn * s
    sh1, sc1, gt1, sh2, sc2, gt2 = ada
    h = _modnorm(x, g_mix.reshape(1, -1), sc1, sh1).reshape(t, d)
    cos, sin = _rope_tables(positions)
    q = _qproj(h, wp["wq"], cos, sin, wp["gq_n"], wp["gq_p"], gn)
    ckv, kpe = _latent_proj(h, wp["w_lat"], cos, sin, wp["g_ckv"], gn)
    z = _mm(h, wp["w_z"], name="proj_z")
    xbc_raw = _mm(h, wp["w_xbc"], name="proj_xbc")
    dt_raw = _mm(h, wp["w_dt"], name="proj_dt")
    gates = _mm(h, wp["w_gates"], name="proj_gates")
    k, v = _keys(ckv, kpe, wp["wuk"], wp["wuv"], wp["gk_n"], wp["gk_p"], gn)
    o_a = attend(q, k, v, ckv)
    y, new_conv, new_ssm = ssm(xbc_raw, dt_raw)
    x1, h2 = _merge(y, z, o_a, gates, x, gt1, sc2, sh2, wp["g_ssm"], wp["g_ffn"],
                    wp["w_br_a"], wp["w_br_b"], wp["w_out"])
    e1, n, e2, r2 = _peer_select(h2.reshape(t, d), wp["peer_wq"], wp["k1"], wp["k2"])
    out = _peer_mix(h2.reshape(t, d), wp["u"], wp["vt"], e1, n, e2, r2, x1, gt2)
    return out, ckv, kpe[:, :D_ROPE], new_conv, new_ssm


def kernel(x_prompt, x_sample, c_prompt, c_sample, cache_ckv, cache_kpe, page_table, state_conv, state_ssm, w_ada, b_ada, g_mix, w_in, g_q, g_ckv, w_uk, w_uv, g_k, conv_w, conv_b, dt_bias, a_log, d_skip, g_ssm, w_br_a, w_br_b, w_out, g_ffn, peer_wq, peer_keys1, peer_keys2, peer_u, peer_v):
    depth = w_ada.shape[0]
    assert depth == 1, "single-layer step"
    b, s, d = x_prompt.shape
    db, ts, _ = x_sample.shape
    assert ts == 1, "one new token per decode sequence"
    past = page_table.shape[1] * cache_ckv.shape[2]
    l = 0
    wp = _prep_weights(w_in[l], g_q[l], g_ckv[l], w_uk[l], w_uv[l], g_k[l], conv_w[l], conv_b[l], dt_bias[l],
                       a_log[l], d_skip[l], g_ssm[l], w_br_a[l], w_br_b[l], w_out[l], g_ffn[l], peer_wq[l],
                       peer_keys1[l], peer_keys2[l], peer_u[l], peer_v[l])

    ada = _ada(jnp.concatenate([c_prompt, c_sample], axis=0), w_ada[l], b_ada[l].reshape(1, -1))
    ada_p = [a.reshape(b, 1, d) for a in jnp.split(ada[:b], N_ADA, axis=-1)]
    ada_s = [a.reshape(1, db, d) for a in jnp.split(ada[b:], N_ADA, axis=-1)]

    def attend_p(q, k, v, ckv):
        return _flash(q, k, v)

    def ssm_p(xbc_raw, dt_raw):
        y, new_ssm = _ssd_prompt(xbc_raw, dt_raw, wp["conv_w"], wp["conv_b"], wp["dt_bias"], wp["a_log"],
                                 wp["d_skip_x"], b)
        new_conv = xbc_raw.reshape(b, s, CONV_DIM)[:, s - (CONV_W - 1):, :]
        return y, new_conv, new_ssm

    yp, ckv_p, kpe_p, conv_p, ssm_p_out = _trunk(x_prompt, ada_p, jnp.arange(s, dtype=I32), wp, g_mix[l],
                                                 attend_p, ssm_p)

    def attend_s(q, k, v, ckv):
        qt, qpe = _qabsorb(q, wp["wukt"], wp["gk_n"], wp["gk_p"])
        olat = _decode_attn(page_table, qt.transpose(1, 0, 2), qpe.transpose(1, 0, 2),
                            q[0].transpose(1, 0, 2), k[0].transpose(1, 0, 2), ckv.reshape(db, 1, D_LATENT),
                            wp["wukt"], cache_ckv[l], cache_kpe[l])
        return _head_out(olat.transpose(1, 0, 2), wp["wuv"])

    def ssm_s(xbc_raw, dt_raw):
        sc = state_conv[l]
        p, hh = SSM_HEADDIM, SSM_HEADS
        to_ph = lambda arr: arr.reshape(arr.shape[:-1] + (hh, p)).swapaxes(-1, -2)
        y_t, new_ssm = _ssm_step(
            to_ph(xbc_raw[:, :D_INNER]), to_ph(sc[:, :, :D_INNER]),
            to_ph(wp["conv_w"][:, :D_INNER]), to_ph(wp["conv_b"][:, :D_INNER])[0],
            xbc_raw[:, D_INNER:].reshape(db, 1, -1), sc[:, :, D_INNER:],
            wp["conv_w"][:, D_INNER:], wp["conv_b"][:, D_INNER:],
            dt_raw.reshape(db, 1, LANES), wp["dt_bias"], wp["a_log"], wp["d_skip"], state_ssm[l])
        y = y_t.swapaxes(1, 2).reshape(db, D_INNER)
        new_conv = jnp.concatenate([sc[:, 1:, :], xbc_raw[:, None, :]], axis=1)
        return y, new_conv, new_ssm

    pos_s = jnp.full((db,), past, dtype=I32)
    ys, ckv_s, kpe_s, conv_s, ssm_s_out = _trunk(x_sample.reshape(1, db, d), ada_s, pos_s, wp, g_mix[l],
                                                 attend_s, ssm_s)

    return (yp, ys.reshape(db, 1, d),
            ckv_p.reshape(1, b, s, D_LATENT), kpe_p.reshape(1, b, s, D_ROPE),
            conv_p[None], ssm_p_out[None],
            ckv_s.reshape(1, db, 1, D_LATENT), kpe_s.reshape(1, db, 1, D_ROPE),
            conv_s[None], ssm_s_out[None])
```

```python
import functools
import math

import numpy as np
import jax
import jax.numpy as jnp
from jax import lax
from jax.experimental import pallas as pl
from jax.experimental.pallas import tpu as pltpu

F32 = jnp.float32
BF16 = jnp.bfloat16
I32 = jnp.int32

D_MODEL = 1024
MLA_HEADS = 8
D_NOPE = 128
D_ROPE = 64
D_QK = D_NOPE + D_ROPE
D_QK_PAD = 256
D_V = 128
D_LATENT = 512
ROPE_THETA = 10000.0
PAGE_SIZE = 128
D_INNER = 2 * D_MODEL
SSM_HEADDIM = 64
SSM_HEADS = D_INNER // SSM_HEADDIM
SSM_GROUPS = 4
HEADS_PER_GROUP = SSM_HEADS // SSM_GROUPS
D_STATE = 128
CONV_W = 4
CONV_DIM = D_INNER + 2 * SSM_GROUPS * D_STATE
SSD_CHUNK = 128
PEER_HEADS = 8
N_KEYS = 128
D_KEY = 256
PEER_TOPK = 16
N_ADA = 6
EPS = 1e-6
IN_SPLITS = (MLA_HEADS * D_QK, D_LATENT, D_ROPE, D_INNER, CONV_DIM, SSM_HEADS, D_MODEL, D_MODEL)
IN_OFFSETS = tuple(int(v) for v in np.cumsum(IN_SPLITS)[:-1])

LANES = 128
SUBLANES = 8
VMEM_LIMIT = 56 * 1024 * 1024
NEG_INF = float("-inf")
NT_DIMS = (((1,), (1,)), ((), ()))


def _params(sem):
    return pltpu.CompilerParams(dimension_semantics=sem, vmem_limit_bytes=VMEM_LIMIT)


def _tile(n, pref):
    t = min(n, pref)
    assert n % t == 0, (n, pref)
    return t


def _sigmoid(x):
    return 1.0 / (1.0 + jnp.exp(-x))


def _silu(x):
    return x * _sigmoid(x)


def _softplus(x):
    return jnp.maximum(x, 0.0) + jnp.log1p(jnp.exp(-jnp.abs(x)))


def _rope_weights(w_pe):
    half = D_ROPE // 2
    w_rot = jnp.concatenate([-w_pe[:, half:], w_pe[:, :half]], axis=1)
    pad = jnp.zeros((w_pe.shape[0], LANES - D_ROPE), w_pe.dtype)
    return jnp.concatenate([w_pe, pad, w_rot, pad], axis=1)


def _rope_tile(pe_rot, cos, sin):
    return pe_rot[:, :LANES] * cos + pe_rot[:, LANES:] * sin


def _ada_kernel(c_ref, w_ref, b_ref, o_ref):
    o_ref[...] = jnp.dot(c_ref[...].astype(BF16), w_ref[...].astype(BF16),
                         preferred_element_type=F32) + b_ref[...]


def _ada(c, w, b):
    r, d = c.shape
    n = w.shape[1]
    tn = _tile(n, 1536)
    return pl.pallas_call(
        _ada_kernel,
        grid=(n // tn,),
        in_specs=[pl.BlockSpec((r, d), lambda j: (0, 0)),
                  pl.BlockSpec((d, tn), lambda j: (0, j)),
                  pl.BlockSpec((1, tn), lambda j: (0, j))],
        out_specs=pl.BlockSpec((r, tn), lambda j: (0, j)),
        out_shape=jax.ShapeDtypeStruct((r, n), F32),
        compiler_params=_params(("arbitrary",)),
        name="ada",
    )(c, w, b)


def _modnorm_kernel(x_ref, g_ref, sc_ref, sh_ref, o_ref):
    x = x_ref[0]
    y = x * lax.rsqrt(jnp.mean(x * x, axis=-1, keepdims=True) + EPS) * g_ref[...]
    o_ref[0] = (y * (1.0 + sc_ref[0]) + sh_ref[0]).astype(o_ref.dtype)


def _mod_spec(arr, ts):
    if arr.shape[1] == 1:
        return pl.BlockSpec((1, 1, arr.shape[2]), lambda b, i: (b, 0, 0))
    return pl.BlockSpec((1, ts, arr.shape[2]), lambda b, i: (b, i, 0))


def _modnorm(x, g, sc, sh):
    gn, s, d = x.shape
    ts = _tile(s, 1024)
    return pl.pallas_call(
        _modnorm_kernel,
        grid=(gn, s // ts),
        in_specs=[pl.BlockSpec((1, ts, d), lambda b, i: (b, i, 0)),
                  pl.BlockSpec((1, d), lambda b, i: (0, 0)),
                  _mod_spec(sc, ts), _mod_spec(sh, ts)],
        out_specs=pl.BlockSpec((1, ts, d), lambda b, i: (b, i, 0)),
        out_shape=jax.ShapeDtypeStruct((gn, s, d), BF16),
        compiler_params=_params(("arbitrary", "arbitrary")),
        name="modnorm",
    )(x, g, sc, sh)


def _qproj_kernel(h_ref, w_ref, cos_ref, sin_ref, gn_ref, gp_ref, o_ref):
    raw = jnp.dot(h_ref[...], w_ref[0], preferred_element_type=F32)
    nope = raw[:, :D_NOPE]
    pe = _rope_tile(raw[:, D_NOPE:], cos_ref[...], sin_ref[...])
    ssq = jnp.sum(nope * nope + pe * pe, axis=-1, keepdims=True)
    r = lax.rsqrt(ssq / D_QK + EPS)
    o_ref[0, 0, :, :D_NOPE] = (nope * r * gn_ref[...]).astype(o_ref.dtype)
    o_ref[0, 0, :, D_NOPE:] = (pe * r * gp_ref[...]).astype(o_ref.dtype)


def _qproj(h, wq, cos, sin, gn, gp, groups):
    t, d = h.shape
    s = t // groups
    tm = _tile(s, 1024)
    nsb = s // tm
    return pl.pallas_call(
        _qproj_kernel,
        grid=(t // tm, MLA_HEADS),
        in_specs=[pl.BlockSpec((tm, d), lambda i, hh: (i, 0)),
                  pl.BlockSpec((1, d, wq.shape[2]), lambda i, hh: (hh, 0, 0)),
                  pl.BlockSpec((tm, LANES), lambda i, hh: (i % nsb, 0)),
                  pl.BlockSpec((tm, LANES), lambda i, hh: (i % nsb, 0)),
                  pl.BlockSpec((1, LANES), lambda i, hh: (0, 0)),
                  pl.BlockSpec((1, LANES), lambda i, hh: (0, 0))],
        out_specs=pl.BlockSpec((1, 1, tm, D_QK_PAD), lambda i, hh: (i // nsb, hh, i % nsb, 0)),
        out_shape=jax.ShapeDtypeStruct((groups, MLA_HEADS, s, D_QK_PAD), BF16),
        compiler_params=_params(("arbitrary", "arbitrary")),
        name="qproj",
    )(h, wq, cos, sin, gn, gp)


def _latent_kernel(h_ref, w_ref, cos_ref, sin_ref, g_ref, ckv_ref, kpe_ref):
    raw = jnp.dot(h_ref[...], w_ref[...], preferred_element_type=F32)
    c = raw[:, :D_LATENT]
    ckv_ref[...] = c * lax.rsqrt(jnp.mean(c * c, axis=-1, keepdims=True) + EPS) * g_ref[...]
    kpe_ref[...] = _rope_tile(raw[:, D_LATENT:], cos_ref[...], sin_ref[...])


def _latent_proj(h, w, cos, sin, g, groups):
    t, d = h.shape
    s = t // groups
    tm = _tile(s, 1024)
    nsb = s // tm
    n = w.shape[1]
    return pl.pallas_call(
        _latent_kernel,
        grid=(t // tm,),
        in_specs=[pl.BlockSpec((tm, d), lambda i: (i, 0)),
                  pl.BlockSpec((d, n), lambda i: (0, 0)),
                  pl.BlockSpec((tm, LANES), lambda i: (i % nsb, 0)),
                  pl.BlockSpec((tm, LANES), lambda i: (i % nsb, 0)),
                  pl.BlockSpec((1, D_LATENT), lambda i: (0, 0))],
        out_specs=[pl.BlockSpec((tm, D_LATENT), lambda i: (i, 0)),
                   pl.BlockSpec((tm, LANES), lambda i: (i, 0))],
        out_shape=[jax.ShapeDtypeStruct((t, D_LATENT), F32),
                   jax.ShapeDtypeStruct((t, LANES), F32)],
        compiler_params=_params(("arbitrary",)),
        name="latent_proj",
    )(h, w, cos, sin, g)


def _mm_kernel(x_ref, w_ref, o_ref):
    o_ref[...] = jnp.dot(x_ref[...], w_ref[...], preferred_element_type=F32).astype(o_ref.dtype)


def _mm(x, w, tm_pref=512, out_dtype=F32, name="mm"):
    t, k = x.shape
    n = w.shape[1]
    tm = _tile(t, tm_pref)
    return pl.pallas_call(
        _mm_kernel,
        grid=(t // tm,),
        in_specs=[pl.BlockSpec((tm, k), lambda i: (i, 0)),
                  pl.BlockSpec((k, n), lambda i: (0, 0))],
        out_specs=pl.BlockSpec((tm, n), lambda i: (i, 0)),
        out_shape=jax.ShapeDtypeStruct((t, n), out_dtype),
        compiler_params=_params(("arbitrary",)),
        name=name,
    )(x, w)


def _keys_kernel(ckv_ref, kpe_ref, wuk_ref, wuv_ref, gn_ref, gp_ref, k_ref, v_ref):
    c = ckv_ref[...].astype(BF16)
    kn = jnp.dot(c, wuk_ref[...], preferred_element_type=F32)
    vv = jnp.dot(c, wuv_ref[...], preferred_element_type=F32)
    pe = kpe_ref[...]
    pss = jnp.sum(pe * pe, axis=-1, keepdims=True)
    for h in range(MLA_HEADS):
        knh = kn[:, h * D_NOPE:(h + 1) * D_NOPE]
        r = lax.rsqrt((jnp.sum(knh * knh, axis=-1, keepdims=True) + pss) / D_QK + EPS)
        k_ref[0, h, :, :D_NOPE] = (knh * r * gn_ref[...]).astype(k_ref.dtype)
        k_ref[0, h, :, D_NOPE:] = (pe * r * gp_ref[...]).astype(k_ref.dtype)
        v_ref[0, h] = vv[:, h * D_V:(h + 1) * D_V].astype(v_ref.dtype)


def _keys(ckv, kpe, wuk, wuv, gn, gp, groups):
    t = ckv.shape[0]
    s = t // groups
    tm = _tile(s, 512)
    nsb = s // tm
    return pl.pallas_call(
        _keys_kernel,
        grid=(groups, nsb),
        in_specs=[pl.BlockSpec((tm, D_LATENT), lambda b, i: (b * nsb + i, 0)),
                  pl.BlockSpec((tm, LANES), lambda b, i: (b * nsb + i, 0)),
                  pl.BlockSpec(wuk.shape, lambda b, i: (0, 0)),
                  pl.BlockSpec(wuv.shape, lambda b, i: (0, 0)),
                  pl.BlockSpec((1, LANES), lambda b, i: (0, 0)),
                  pl.BlockSpec((1, LANES), lambda b, i: (0, 0))],
        out_specs=[pl.BlockSpec((1, MLA_HEADS, tm, D_QK_PAD), lambda b, i: (b, 0, i, 0)),
                   pl.BlockSpec((1, MLA_HEADS, tm, D_V), lambda b, i: (b, 0, i, 0))],
        out_shape=[jax.ShapeDtypeStruct((groups, MLA_HEADS, s, D_QK_PAD), BF16),
                   jax.ShapeDtypeStruct((groups, MLA_HEADS, s, D_V), BF16)],
        compiler_params=_params(("arbitrary", "arbitrary")),
        name="mla_keys",
    )(ckv, kpe, wuk, wuv, gn, gp)


def _flash_kernel(q_ref, k_ref, v_ref, o_ref, *, tq, scale):
    qi = pl.program_id(2)
    q = q_ref[0, 0]

    def step(j, carry, masked):
        m, l, acc = carry
        start = pl.multiple_of(j * tq, tq)
        k = k_ref[0, 0, pl.ds(start, tq), :]
        v = v_ref[0, 0, pl.ds(start, tq), :]
        s = lax.dot_general(q, k, NT_DIMS, preferred_element_type=F32) * scale
        if masked:
            row = lax.broadcasted_iota(I32, (tq, tq), 0)
            col = lax.broadcasted_iota(I32, (tq, tq), 1)
            s = jnp.where(col <= row, s, NEG_INF)
        m_new = jnp.maximum(m, jnp.max(s, axis=-1, keepdims=True))
        alpha = jnp.exp(m - m_new)
        p = jnp.exp(s - m_new)
        l = l * alpha + jnp.sum(p, axis=-1, keepdims=True)
        acc = acc * alpha + jnp.dot(p.astype(BF16), v, preferred_element_type=F32)
        return m_new, l, acc

    init = (jnp.full((tq, 1), NEG_INF, F32), jnp.zeros((tq, 1), F32), jnp.zeros((tq, D_V), F32))
    carry = lax.fori_loop(0, qi, lambda j, c: step(j, c, False), init)
    _, l, acc = step(qi, carry, True)
    o_ref[...] = (acc / l).astype(o_ref.dtype)


def _flash(q, k, v):
    b, hh, s, _ = q.shape
    tq = _tile(s, 512)
    nq = s // tq
    return pl.pallas_call(
        functools.partial(_flash_kernel, tq=tq, scale=D_QK ** -0.5),
        grid=(b, hh, nq),
        in_specs=[pl.BlockSpec((1, 1, tq, D_QK_PAD), lambda bi, h, qi: (bi, h, qi, 0)),
                  pl.BlockSpec((1, 1, s, D_QK_PAD), lambda bi, h, qi: (bi, h, 0, 0)),
                  pl.BlockSpec((1, 1, s, D_V), lambda bi, h, qi: (bi, h, 0, 0))],
        out_specs=pl.BlockSpec((tq, D_V), lambda bi, h, qi: (bi * nq + qi, h)),
        out_shape=jax.ShapeDtypeStruct((b * s, hh * D_V), BF16),
        compiler_params=_params(("arbitrary", "arbitrary", "arbitrary")),
        name="flash_prompt",
    )(q, k, v)


def _qabsorb_kernel(q_ref, wukt_ref, gn_ref, gp_ref, qt_ref, qpe_ref):
    q = q_ref[0, 0].astype(F32)
    qg = (q[:, :D_NOPE] * gn_ref[...]).astype(BF16)
    qt_ref[0] = jnp.dot(qg, wukt_ref[...], preferred_element_type=F32).astype(qt_ref.dtype)
    qpe_ref[0] = (q[:, D_NOPE:] * gp_ref[...]).astype(qpe_ref.dtype)


def _qabsorb(q, wukt, gn, gp):
    db = q.shape[2]
    return pl.pallas_call(
        _qabsorb_kernel,
        grid=(MLA_HEADS,),
        in_specs=[pl.BlockSpec((1, 1, db, D_QK_PAD), lambda h: (0, h, 0, 0)),
                  pl.BlockSpec((D_NOPE, D_LATENT), lambda h: (h, 0)),
                  pl.BlockSpec((1, LANES), lambda h: (0, 0)),
                  pl.BlockSpec((1, LANES), lambda h: (0, 0))],
        out_specs=[pl.BlockSpec((1, db, D_LATENT), lambda h: (h, 0, 0)),
                   pl.BlockSpec((1, db, LANES), lambda h: (h, 0, 0))],
        out_shape=[jax.ShapeDtypeStruct((MLA_HEADS, db, D_LATENT), BF16),
                   jax.ShapeDtypeStruct((MLA_HEADS, db, LANES), BF16)],
        compiler_params=_params(("arbitrary",)),
        name="q_absorb",
    )(q, wukt, gn, gp)


QT_ROWS = 16


def _page_copies(pt_ref, cache_ckv, cache_kpt, ckv_buf, kpt_buf, sem, chunk, slot, pages_per_chunk):
    copies = []
    for p in range(pages_per_chunk):
        page = pt_ref[chunk * pages_per_chunk + p]
        keys = pl.ds(p * PAGE_SIZE, PAGE_SIZE)
        copies.append(pltpu.make_async_copy(cache_ckv.at[page], ckv_buf.at[slot, keys, :], sem.at[0, slot]))
        copies.append(pltpu.make_async_copy(cache_kpt.at[page], kpt_buf.at[slot, :, keys], sem.at[1, slot]))
    return copies


def _decode_kernel(pt_ref, qt_ref, qpe_ref, q_ref, knew_ref, cnew_ref, wukt_ref, cache_ckv, cache_kpt,
                   o_ref, ckv_buf, kpt_buf, sem, lhs_ref, m_sc, l_sc, acc_sc, *, pages_per_chunk, scale):
    s_id = pl.program_id(0)
    c_id = pl.program_id(1)
    nc = pl.num_programs(1)
    step = s_id * nc + c_id
    total = pl.num_programs(0) * nc
    slot = step % 2
    copy_args = (pt_ref, cache_ckv, cache_kpt, ckv_buf, kpt_buf, sem)
    n_up = MLA_HEADS * D_NOPE

    @pl.when(step == 0)
    def _():
        lhs_ref[:n_up, :] = wukt_ref[...]
        for cp in _page_copies(*copy_args, 0, 0, pages_per_chunk):
            cp.start()

    @pl.when(step + 1 < total)
    def _():
        for cp in _page_copies(*copy_args, step + 1, 1 - slot, pages_per_chunk):
            cp.start()

    @pl.when(c_id == 0)
    def _():
        lhs_ref[n_up:, :] = qt_ref[0]
        s_new = jnp.sum(q_ref[0].astype(F32) * knew_ref[0].astype(F32), axis=-1, keepdims=True) * scale
        m_sc[...] = jnp.broadcast_to(s_new, m_sc.shape)
        l_sc[...] = jnp.ones(l_sc.shape, F32)
        acc_sc[...] = jnp.broadcast_to(cnew_ref[0], acc_sc.shape)

    for cp in _page_copies(*copy_args, step, slot, pages_per_chunk):
        cp.wait()

    cb = ckv_buf[slot].astype(BF16)
    tk = cb.shape[0]
    big = lax.dot_general(lhs_ref[...], cb, NT_DIMS, preferred_element_type=F32)
    hrow = lax.broadcasted_iota(I32, (MLA_HEADS, tk), 0)
    ssq = jnp.zeros((MLA_HEADS, tk), F32)
    for h in range(MLA_HEADS):
        blk = big[h * D_NOPE:(h + 1) * D_NOPE, :]
        ssq = jnp.where(hrow == h, jnp.sum(blk * blk, axis=0, keepdims=True), ssq)
    sn = big[n_up:n_up + MLA_HEADS, :]
    kpt = kpt_buf[slot]
    pss = jnp.sum(kpt * kpt, axis=0, keepdims=True)
    spe = jnp.dot(qpe_ref[0][:, :D_ROPE], kpt.astype(BF16), preferred_element_type=F32)
    s = (sn + spe) * lax.rsqrt((ssq + pss) / D_QK + EPS) * scale

    m_old = m_sc[:, :1]
    m_new = jnp.maximum(m_old, jnp.max(s, axis=-1, keepdims=True))
    alpha = jnp.exp(m_old - m_new)
    p = jnp.exp(s - m_new)
    l_new = l_sc[:, :1] * alpha + jnp.sum(p, axis=-1, keepdims=True)
    acc_new = acc_sc[...] * alpha + jnp.dot(p.astype(BF16), cb, preferred_element_type=F32)
    m_sc[...] = jnp.broadcast_to(m_new, m_sc.shape)
    l_sc[...] = jnp.broadcast_to(l_new, l_sc.shape)
    acc_sc[...] = acc_new

    @pl.when(c_id == nc - 1)
    def _():
        o_ref[0] = acc_new / l_new


def _decode_attn(page_table, qt, qpe, q, knew, cnew, wukt, cache_ckv, cache_kpt):
    db, n_pages = page_table.shape
    ppc = _tile(n_pages, 16)
    nc = n_pages // ppc
    tk = ppc * PAGE_SIZE
    n_lhs = MLA_HEADS * D_NOPE + QT_ROWS
    grid_spec = pltpu.PrefetchScalarGridSpec(
        num_scalar_prefetch=1,
        grid=(db, nc),
        in_specs=[pl.BlockSpec((1, QT_ROWS, D_LATENT), lambda s, c, pt: (s, 0, 0)),
                  pl.BlockSpec((1, MLA_HEADS, LANES), lambda s, c, pt: (s, 0, 0)),
                  pl.BlockSpec((1, MLA_HEADS, D_QK_PAD), lambda s, c, pt: (s, 0, 0)),
                  pl.BlockSpec((1, MLA_HEADS, D_QK_PAD), lambda s, c, pt: (s, 0, 0)),
                  pl.BlockSpec((1, 1, D_LATENT), lambda s, c, pt: (s, 0, 0)),
                  pl.BlockSpec(wukt.shape, lambda s, c, pt: (0, 0)),
                  pl.BlockSpec(memory_space=pl.ANY),
                  pl.BlockSpec(memory_space=pl.ANY)],
        out_specs=pl.BlockSpec((1, MLA_HEADS, D_LATENT), lambda s, c, pt: (s, 0, 0)),
        scratch_shapes=[pltpu.VMEM((2, tk, D_LATENT), F32),
                        pltpu.VMEM((2, D_ROPE, tk), F32),
                        pltpu.SemaphoreType.DMA((2, 2)),
                        pltpu.VMEM((n_lhs, D_LATENT), BF16),
                        pltpu.VMEM((MLA_HEADS, LANES), F32),
                        pltpu.VMEM((MLA_HEADS, LANES), F32),
                        pltpu.VMEM((MLA_HEADS, D_LATENT), F32)],
    )
    return pl.pallas_call(
        functools.partial(_decode_kernel, pages_per_chunk=ppc, scale=D_QK ** -0.5),
        grid_spec=grid_spec,
        out_shape=jax.ShapeDtypeStruct((db, MLA_HEADS, D_LATENT), F32),
        compiler_params=_params(("arbitrary", "arbitrary")),
        name="decode_attn",
    )(page_table.reshape(-1), qt, qpe, q, knew, cnew, wukt, cache_ckv, cache_kpt)


def _headmm_kernel(x_ref, w_ref, o_ref):
    o_ref[...] = jnp.dot(x_ref[0].astype(BF16), w_ref[...], preferred_element_type=F32).astype(o_ref.dtype)


def _head_out(olat, wuv):
    hh, db, _ = olat.shape
    return pl.pallas_call(
        _headmm_kernel,
        grid=(hh,),
        in_specs=[pl.BlockSpec((1, db, D_LATENT), lambda h: (h, 0, 0)),
                  pl.BlockSpec((D_LATENT, D_V), lambda h: (0, h))],
        out_specs=pl.BlockSpec((db, D_V), lambda h: (0, h)),
        out_shape=jax.ShapeDtypeStruct((db, hh * D_V), BF16),
        compiler_params=_params(("arbitrary",)),
        name="head_out",
    )(olat, wuv)


def _ssd_kernel(xbc_ref, dt_ref, cw_ref, cb_ref, dtb_ref, alog_ref, dsk_ref, y_ref, ssm_ref,
                carry_ref, ht_ref):
    c_id = pl.program_id(1)
    q = SSD_CHUNK

    @pl.when(c_id == 0)
    def _():
        carry_ref[...] = jnp.zeros(carry_ref.shape, F32)
        ht_ref[...] = jnp.zeros(ht_ref.shape, F32)

    cur = xbc_ref[...]
    ext = jnp.concatenate([carry_ref[...], cur], axis=0)
    acc = cb_ref[...] + cur * cw_ref[CONV_W - 1:CONV_W, :]
    for sft in range(1, CONV_W):
        acc = acc + pltpu.roll(ext, sft, 0)[SUBLANES:, :] * cw_ref[CONV_W - 1 - sft:CONV_W - sft, :]
    xbc = _silu(acc)
    carry_ref[...] = cur[q - SUBLANES:, :]

    dtv = _softplus(dt_ref[...] + dtb_ref[...])
    a = -jnp.exp(alog_ref[...])
    row = lax.broadcasted_iota(I32, (q, LANES), 0)
    acs = dtv * a
    sft = 1
    while sft < q:
        acs = acs + jnp.where(row >= sft, pltpu.roll(acs, sft, 0), 0.0)
        sft *= 2
    acs_t = acs.T
    dt_t = dtv.T
    acs_last = acs[q - 1:q, :]
    to_end = jnp.exp(acs_last - acs) * dtv
    e_acs = jnp.exp(acs)
    e_last = jnp.exp(acs_last)

    col = lax.broadcasted_iota(I32, (q, q), 1)
    causal = lax.broadcasted_iota(I32, (q, q), 0) >= col
    left = col < SSM_HEADDIM
    left_row = left[:1, :]
    b_off = D_INNER
    c_off = D_INNER + SSM_GROUPS * D_STATE
    for g in range(SSM_GROUPS):
        bg = xbc[:, b_off + g * D_STATE:b_off + (g + 1) * D_STATE]
        cg = xbc[:, c_off + g * D_STATE:c_off + (g + 1) * D_STATE].astype(BF16)
        cbm = lax.dot_general(cg, bg.astype(BF16), NT_DIMS, preferred_element_type=F32)
        htg = ht_ref[g]
        y_off = jnp.dot(cg, htg.astype(BF16), preferred_element_type=F32)
        xte = []
        decay = []
        for pr in range(HEADS_PER_GROUP // 2):
            h0 = g * HEADS_PER_GROUP + 2 * pr
            lanes = slice(h0 * SSM_HEADDIM, (h0 + 2) * SSM_HEADDIM)
            xp = xbc[:, lanes]
            xpb = xp.astype(BF16)
            yd = []
            for h in (h0, h0 + 1):
                seg = acs[:, h:h + 1] - acs_t[h:h + 1, :]
                w = cbm * jnp.exp(jnp.where(causal, seg, NEG_INF)) * dt_t[h:h + 1, :]
                yd.append(jnp.dot(w.astype(BF16), xpb, preferred_element_type=F32))
            e_pair = jnp.where(left, e_acs[:, h0:h0 + 1], e_acs[:, h0 + 1:h0 + 2])
            y_pair = (jnp.where(left, yd[0], yd[1])
                      + y_off[:, pr * LANES:(pr + 1) * LANES] * e_pair
                      + dsk_ref[:, lanes] * xp)
            y_ref[:, lanes] = y_pair
            te_pair = jnp.where(left, to_end[:, h0:h0 + 1], to_end[:, h0 + 1:h0 + 2])
            xte.append((xp * te_pair).astype(BF16))
            decay.append(jnp.where(left_row, e_last[:, h0:h0 + 1], e_last[:, h0 + 1:h0 + 2]))
        xte = jnp.concatenate(xte, axis=1)
        decay = jnp.concatenate(decay, axis=1)
        ht_ref[g] = htg * decay + jnp.dot(bg.T.astype(BF16), xte, preferred_element_type=F32)

    @pl.when(c_id == pl.num_programs(1) - 1)
    def _():
        for g in range(SSM_GROUPS):
            ssm_ref[0, g * HEADS_PER_GROUP:(g + 1) * HEADS_PER_GROUP] = ht_ref[g].T.reshape(
                HEADS_PER_GROUP, SSM_HEADDIM, D_STATE)


def _ssd_prompt(xbc_raw, dt_raw, cw, cb, dtb, alog, dsk, batch):
    t = xbc_raw.shape[0]
    s = t // batch
    assert s % SSD_CHUNK == 0
    nc = s // SSD_CHUNK
    full = lambda arr: pl.BlockSpec(arr.shape, lambda b, c: (0, 0))
    return pl.pallas_call(
        _ssd_kernel,
        grid=(batch, nc),
        in_specs=[pl.BlockSpec((SSD_CHUNK, CONV_DIM), lambda b, c: (b * nc + c, 0)),
                  pl.BlockSpec((SSD_CHUNK, LANES), lambda b, c: (b * nc + c, 0)),
                  full(cw), full(cb), full(dtb), full(alog), full(dsk)],
        out_specs=[pl.BlockSpec((SSD_CHUNK, D_INNER), lambda b, c: (b * nc + c, 0)),
                   pl.BlockSpec((1, SSM_HEADS, SSM_HEADDIM, D_STATE), lambda b, c: (b, 0, 0, 0))],
        out_shape=[jax.ShapeDtypeStruct((t, D_INNER), F32),
                   jax.ShapeDtypeStruct((batch, SSM_HEADS, SSM_HEADDIM, D_STATE), F32)],
        scratch_shapes=[pltpu.VMEM((SUBLANES, CONV_DIM), F32),
                        pltpu.VMEM((SSM_GROUPS, D_STATE, HEADS_PER_GROUP * SSM_HEADDIM), F32)],
        compiler_params=_params(("arbitrary", "arbitrary")),
        name="ssd_prompt",
    )(xbc_raw, dt_raw, cw, cb, dtb, alog, dsk)


def _ssm_step_kernel(xr_ref, sx_ref, cwx_ref, cbx_ref, bcr_ref, sbc_ref, cwbc_ref, cbbc_ref,
                     dt_ref, dtb_ref, alog_ref, dsk_ref, st_ref, y_ref, nst_ref):
    xt = cbx_ref[...] + cwx_ref[CONV_W - 1] * xr_ref[0]
    bc = cbbc_ref[...] + cwbc_ref[CONV_W - 1:CONV_W, :] * bcr_ref[0]
    for k in range(CONV_W - 1):
        xt = xt + cwx_ref[k] * sx_ref[0, k]
        bc = bc + cwbc_ref[k:k + 1, :] * sbc_ref[0, k:k + 1, :]
    xt = _silu(xt)
    bc = _silu(bc)
    dtv = _softplus(dt_ref[0] + dtb_ref[...])[:, :SSM_HEADS]
    d_a = jnp.exp(dtv * (-jnp.exp(alog_ref[...]))[:, :SSM_HEADS])
    xdt = xt * dtv
    d_a = jnp.broadcast_to(d_a, xt.shape)
    lane = lax.broadcasted_iota(I32, xt.shape, 1)
    y_t = jnp.zeros(xt.shape, F32)
    for h in range(SSM_HEADS):
        g = h // HEADS_PER_GROUP
        bg = bc[:, g * D_STATE:(g + 1) * D_STATE]
        cg = bc[:, (SSM_GROUPS + g) * D_STATE:(SSM_GROUPS + g + 1) * D_STATE]
        hn = d_a[:, h:h + 1] * st_ref[0, h] + xdt[:, h:h + 1] * bg
        nst_ref[0, h] = hn
        y_t = jnp.where(lane == h, jnp.sum(hn * cg, axis=-1, keepdims=True), y_t)
    y_ref[0] = y_t + dsk_ref[...][:, :SSM_HEADS] * xt


def _ssm_step(xr_t, sx_t, cwx_t, cbx_t, bcr, sbc, cwbc, cbbc, dt_raw, dtb, alog, dsk, state):
    db = state.shape[0]
    p, hh = SSM_HEADDIM, SSM_HEADS
    nbc = 2 * SSM_GROUPS * D_STATE
    c2 = lambda arr: pl.BlockSpec(arr.shape, lambda s: (0,) * arr.ndim)
    return pl.pallas_call(
        _ssm_step_kernel,
        grid=(db,),
        in_specs=[pl.BlockSpec((1, p, hh), lambda s: (s, 0, 0)),
                  pl.BlockSpec((1, CONV_W - 1, p, hh), lambda s: (s, 0, 0, 0)),
                  c2(cwx_t), c2(cbx_t),
                  pl.BlockSpec((1, 1, nbc), lambda s: (s, 0, 0)),
                  pl.BlockSpec((1, CONV_W - 1, nbc), lambda s: (s, 0, 0)),
                  c2(cwbc), c2(cbbc),
                  pl.BlockSpec((1, 1, LANES), lambda s: (s, 0, 0)),
                  c2(dtb), c2(alog), c2(dsk),
                  pl.BlockSpec((1, hh, p, D_STATE), lambda s: (s, 0, 0, 0))],
        out_specs=[pl.BlockSpec((1, p, hh), lambda s: (s, 0, 0)),
                   pl.BlockSpec((1, hh, p, D_STATE), lambda s: (s, 0, 0, 0))],
        out_shape=[jax.ShapeDtypeStruct((db, p, hh), F32),
                   jax.ShapeDtypeStruct(state.shape, F32)],
        compiler_params=_params(("arbitrary",)),
        name="ssm_step",
    )(xr_t, sx_t, cwx_t, cbx_t, bcr, sbc, cwbc, cbbc, dt_raw, dtb, alog, dsk, state)


def _merge_kernel(y_ref, z_ref, oa_ref, gab_a_ref, gab_b_ref, x_ref, gt1_ref, sc2_ref, sh2_ref,
                  gssm_ref, gffn_ref, wa_ref, wb_ref, wo_ref, x1_ref, h2_ref):
    yz = y_ref[...] * _silu(z_ref[...])
    yb = yz * lax.rsqrt(jnp.mean(yz * yz, axis=-1, keepdims=True) + EPS) * gssm_ref[...]
    ma = jnp.dot(oa_ref[...], wa_ref[...], preferred_element_type=F32)
    mb = jnp.dot(yb.astype(BF16), wb_ref[...], preferred_element_type=F32)
    merged = _sigmoid(gab_a_ref[...]) * ma + _sigmoid(gab_b_ref[...]) * mb
    x1 = x_ref[0] + gt1_ref[0] * jnp.dot(merged.astype(BF16), wo_ref[...], preferred_element_type=F32)
    x1_ref[0] = x1
    hn = x1 * lax.rsqrt(jnp.mean(x1 * x1, axis=-1, keepdims=True) + EPS) * gffn_ref[...]
    h2_ref[0] = (hn * (1.0 + sc2_ref[0]) + sh2_ref[0]).astype(h2_ref.dtype)


def _merge(y, z, oa, gates, x, gt1, sc2, sh2, gssm, gffn, wa, wb, wo):
    gn, s, d = x.shape
    ts = _tile(s, 256)
    nsb = s // ts
    tok = lambda w: pl.BlockSpec((ts, w), lambda b, i: (b * nsb + i, 0))
    c2 = lambda arr: pl.BlockSpec(arr.shape, lambda b, i: (0, 0))
    return pl.pallas_call(
        _merge_kernel,
        grid=(gn, nsb),
        in_specs=[tok(D_INNER), tok(D_INNER), tok(d),
                  pl.BlockSpec((ts, d), lambda b, i: (b * nsb + i, 0)),
                  pl.BlockSpec((ts, d), lambda b, i: (b * nsb + i, 1)),
                  pl.BlockSpec((1, ts, d), lambda b, i: (b, i, 0)),
                  _mod_spec(gt1, ts), _mod_spec(sc2, ts), _mod_spec(sh2, ts),
                  c2(gssm), c2(gffn), c2(wa), c2(wb), c2(wo)],
        out_specs=[pl.BlockSpec((1, ts, d), lambda b, i: (b, i, 0)),
                   pl.BlockSpec((1, ts, d), lambda b, i: (b, i, 0))],
        out_shape=[jax.ShapeDtypeStruct((gn, s, d), F32),
                   jax.ShapeDtypeStruct((gn, s, d), BF16)],
        compiler_params=_params(("arbitrary", "arbitrary")),
        name="merge",
    )(y, z, oa, gates, gates, x, gt1, sc2, sh2, gssm, gffn, wa, wb, wo)


N_CAND_ROWS = 80


def _cand_tables():
    flat = np.zeros((N_CAND_ROWS,), np.int32)
    valid = np.zeros((N_CAND_ROWS,), bool)
    for r in range(8):
        flat[r], valid[r] = r, True
        flat[8 + r], valid[8 + r] = 8 + r, True
        flat[16 + r], valid[16 + r] = 16 + r, True
        for ra in range(2, 8):
            flat[8 * (ra + 1) + r] = ra * 16 + r
            valid[8 * (ra + 1) + r] = r < 16 // (ra + 1)
        flat[72 + r], valid[72 + r] = (8 + r) * 16, True
    return flat, valid


def _top16_rows(s):
    n, tm = s.shape
    iota = lax.broadcasted_iota(I32, (n, tm), 0).astype(F32)
    r16 = lax.broadcasted_iota(I32, (PEER_TOPK, tm), 0)
    rank = jnp.full((n, tm), 255.0, F32)
    vals = jnp.zeros((PEER_TOPK, tm), F32)
    for r in range(PEER_TOPK):
        m = jnp.max(s, axis=0, keepdims=True)
        idx = jnp.min(jnp.where(s == m, iota, float(n)), axis=0, keepdims=True)
        hit = iota == idx
        rank = jnp.where(hit, float(r), rank)
        s = jnp.where(hit, NEG_INF, s)
        vals = jnp.where(r16 == r, m, vals)
    return vals, rank


def _peer_select_kernel(h2_ref, wq_ref, k1_ref, k2_ref, flat_ref, valid_ref,
                        e1_ref, n_ref, e2_ref, r2_ref):
    qh = jnp.dot(h2_ref[...], wq_ref[...], preferred_element_type=F32)
    tm = qh.shape[0]
    half = D_KEY // 2
    flat = flat_ref[...]
    valid = valid_ref[...] > 0
    row8 = lax.broadcasted_iota(I32, (SUBLANES, tm), 0)
    for h in range(PEER_HEADS):
        q1 = qh[:, h * D_KEY:h * D_KEY + half].astype(BF16)
        q2 = qh[:, h * D_KEY + half:(h + 1) * D_KEY].astype(BF16)
        s1 = lax.dot_general(k1_ref[h], q1, NT_DIMS, preferred_element_type=F32)
        s2 = lax.dot_general(k2_ref[h], q2, NT_DIMS, preferred_element_type=F32)
        v1, rank1 = _top16_rows(s1)
        v2, rank2 = _top16_rows(s2)
        v2_lo = v2[:SUBLANES]
        groups = [v1[0:1] + v2_lo, v1[0:1] + v2[SUBLANES:], v1[1:2] + v2_lo]
        groups += [v1[ra:ra + 1] + v2_lo for ra in range(2, 8)]
        groups.append(v1[SUBLANES:] + v2[0:1])
        cand = jnp.where(valid, jnp.concatenate(groups, axis=0), NEG_INF)
        work = cand
        self32 = jnp.zeros(cand.shape, F32)
        for _ in range(PEER_TOPK):
            m = jnp.max(work, axis=0, keepdims=True)
            pick = jnp.min(jnp.where(work == m, flat, 1e6), axis=0, keepdims=True)
            hit = flat == pick
            self32 = jnp.where(hit, 1.0, self32)
            work = jnp.where(hit, NEG_INF, work)
        top = v1[0:1] + v2[0:1]
        z = jnp.sum(jnp.where(self32 > 0.0, jnp.exp(cand - top), 0.0), axis=0, keepdims=True)
        cnt_lo = jnp.zeros((SUBLANES, tm), F32)
        per_group = [self32[0:8] + self32[8:16]] + [self32[8 * gidx:8 * gidx + 8] for gidx in range(2, 9)]
        for ra, blk in enumerate(per_group):
            cnt_lo = jnp.where(row8 == ra, jnp.sum(blk, axis=0, keepdims=True), cnt_lo)
        cnt = jnp.concatenate([cnt_lo, self32[72:80]], axis=0)
        n_key = jnp.zeros(s1.shape, F32)
        for ra in range(PEER_TOPK):
            n_key = jnp.where(rank1 == float(ra), cnt[ra:ra + 1], n_key)
        e1_ref[h] = jnp.where(rank1 < PEER_TOPK, jnp.exp(s1 - v1[0:1]), 0.0) / z
        n_ref[h] = n_key
        e2_ref[h] = jnp.where(rank2 < PEER_TOPK, jnp.exp(s2 - v2[0:1]), 0.0).astype(e2_ref.dtype)
        r2_ref[h] = rank2.astype(r2_ref.dtype)


def _peer_select(h2, wq, k1, k2):
    t, d = h2.shape
    tm = _tile(t, 256)
    flat, valid = _cand_tables()
    flat_t = jnp.asarray(np.broadcast_to(flat[:, None].astype(np.float32), (N_CAND_ROWS, tm)).copy())
    valid_t = jnp.asarray(np.broadcast_to(valid[:, None].astype(np.int32), (N_CAND_ROWS, tm)).copy())
    c2 = lambda arr: pl.BlockSpec(arr.shape, lambda i: (0,) * arr.ndim)
    tab = pl.BlockSpec((PEER_HEADS, N_KEYS, tm), lambda i: (0, 0, i))
    shp = jax.ShapeDtypeStruct((PEER_HEADS, N_KEYS, t), F32)
    shp_b = jax.ShapeDtypeStruct((PEER_HEADS, N_KEYS, t), BF16)
    return pl.pallas_call(
        _peer_select_kernel,
        grid=(t // tm,),
        in_specs=[pl.BlockSpec((tm, d), lambda i: (i, 0)), c2(wq), c2(k1), c2(k2), c2(flat_t), c2(valid_t)],
        out_specs=[tab, tab, tab, tab],
        out_shape=[shp, shp, shp_b, shp_b],
        compiler_params=_params(("arbitrary",)),
        name="peer_select",
    )(h2, wq, k1, k2, flat_t, valid_t)


A_PER_BLOCK = 8
E_PER_BLOCK = A_PER_BLOCK * N_KEYS


def _peer_mix_kernel(h2_ref, u_ref, vt_ref, e1_ref, n_ref, e2_ref, r2_ref, x1_ref, gt2_ref,
                     o_ref, acc_ref, w_ref):
    j = pl.program_id(2)

    @pl.when(j == 0)
    def _():
        acc_ref[...] = jnp.zeros(acc_ref.shape, F32)

    st = lax.dot_general(u_ref[...], h2_ref[...], NT_DIMS, preferred_element_type=F32)
    tm = st.shape[1]
    pack = 2 * SUBLANES
    tiles = (N_KEYS // pack, pack, tm)
    for al in range(A_PER_BLOCK):
        sa = st[al * N_KEYS:(al + 1) * N_KEYS, :]
        gate = jnp.zeros(tiles, BF16)
        for h in range(PEER_HEADS):
            n_row = jnp.broadcast_to(n_ref[h, al:al + 1, :], (pack, tm)).astype(BF16)
            e1_row = jnp.broadcast_to(e1_ref[h, al:al + 1, :], (pack, tm)).astype(BF16)
            keep = r2_ref[h].reshape(tiles) < n_row[None]
            gate = gate + e1_row[None] * jnp.where(keep, e2_ref[h].reshape(tiles), jnp.zeros((), BF16))
        act = (0.5 * sa * (1.0 + lax.erf(sa * (2.0 ** -0.5)))).astype(BF16)
        w_ref[al * N_KEYS:(al + 1) * N_KEYS, :] = gate.reshape(N_KEYS, tm) * act
    acc_ref[...] += jnp.dot(vt_ref[...], w_ref[...], preferred_element_type=F32)

    @pl.when(j == pl.num_programs(2) - 1)
    def _():
        o_ref[0] = x1_ref[0] + gt2_ref[0] * acc_ref[...].T


def _peer_mix(h2, u, vt, e1, n, e2, r2, x1, gt2):
    gn, s, d = x1.shape
    ts = _tile(s, 512)
    nsb = s // ts
    ne = u.shape[0] // E_PER_BLOCK
    rows = pl.BlockSpec((PEER_HEADS, A_PER_BLOCK, ts), lambda b, i, j: (0, j, b * nsb + i))
    full = pl.BlockSpec((PEER_HEADS, N_KEYS, ts), lambda b, i, j: (0, 0, b * nsb + i))

    def gt_spec(arr):
        if arr.shape[1] == 1:
            return pl.BlockSpec((1, 1, d), lambda b, i, j: (b, 0, 0))
        return pl.BlockSpec((1, ts, d), lambda b, i, j: (b, i, 0))

    return pl.pallas_call(
        _peer_mix_kernel,
        grid=(gn, nsb, ne),
        in_specs=[pl.BlockSpec((ts, d), lambda b, i, j: (b * nsb + i, 0)),
                  pl.BlockSpec((E_PER_BLOCK, d), lambda b, i, j: (j, 0)),
                  pl.BlockSpec((d, E_PER_BLOCK), lambda b, i, j: (0, j)),
                  rows, rows, full, full,
                  pl.BlockSpec((1, ts, d), lambda b, i, j: (b, i, 0)),
                  gt_spec(gt2)],
        out_specs=pl.BlockSpec((1, ts, d), lambda b, i, j: (b, i, 0)),
        out_shape=jax.ShapeDtypeStruct((gn, s, d), F32),
        scratch_shapes=[pltpu.VMEM((d, ts), F32), pltpu.VMEM((E_PER_BLOCK, ts), BF16)],
        compiler_params=_params(("arbitrary", "arbitrary", "arbitrary")),
        name="peer_mix",
    )(h2, u, vt, e1, n, e2, r2, x1, gt2)


def _rope_tables(positions):
    half = D_ROPE // 2
    inv = ROPE_THETA ** (-jnp.arange(half, dtype=F32) / half)
    ang = positions.astype(F32)[:, None] * inv[None, :]
    pad = jnp.zeros((positions.shape[0], LANES - D_ROPE), F32)
    cos = jnp.cos(ang)
    sin = jnp.sin(ang)
    return jnp.concatenate([cos, cos, pad], axis=1), jnp.concatenate([sin, sin, pad], axis=1)


def _pad_lanes(v, width=LANES):
    v = v.reshape(1, -1)
    return jnp.pad(v, ((0, 0), (0, width - v.shape[1])))


def _prep_weights(w_in, g_q, g_ckv, w_uk, w_uv, g_k, conv_w, conv_b, dt_bias, a_log, d_skip, g_ssm,
                  w_br_a, w_br_b, w_out, g_ffn, peer_wq, peer_keys1, peer_keys2, peer_u, peer_v):
    d = D_MODEL
    o = IN_OFFSETS
    wq = w_in[:, :o[0]].reshape(d, MLA_HEADS, D_QK).transpose(1, 0, 2)
    wq = jnp.concatenate([wq[:, :, :D_NOPE], jax.vmap(_rope_weights)(wq[:, :, D_NOPE:])], axis=2).astype(BF16)
    w_lat = jnp.concatenate([w_in[:, o[0]:o[1]], _rope_weights(w_in[:, o[1]:o[2]])], axis=1).astype(BF16)
    w_dt = jnp.pad(w_in[:, o[4]:o[5]], ((0, 0), (0, LANES - SSM_HEADS))).astype(BF16)
    wuk2 = w_uk.reshape(D_LATENT, MLA_HEADS * D_NOPE)
    return dict(
        wq=wq, w_lat=w_lat,
        w_z=w_in[:, o[2]:o[3]].astype(BF16),
        w_xbc=w_in[:, o[3]:o[4]].astype(BF16),
        w_dt=w_dt,
        w_gates=w_in[:, o[5]:].astype(BF16),
        gq_n=g_q[:D_NOPE].reshape(1, -1), gq_p=_pad_lanes(g_q[D_NOPE:]),
        gk_n=g_k[:D_NOPE].reshape(1, -1), gk_p=_pad_lanes(g_k[D_NOPE:]),
        g_ckv=g_ckv.reshape(1, -1),
        wuk=wuk2.astype(BF16), wukt=wuk2.T.astype(BF16),
        wuv=w_uv.reshape(D_LATENT, MLA_HEADS * D_V).astype(BF16),
        conv_w=conv_w, conv_b=conv_b.reshape(1, -1),
        dt_bias=_pad_lanes(dt_bias), a_log=_pad_lanes(a_log), d_skip=_pad_lanes(d_skip),
        d_skip_x=jnp.repeat(d_skip, SSM_HEADDIM).reshape(1, -1),
        g_ssm=g_ssm.reshape(1, -1), g_ffn=g_ffn.reshape(1, -1),
        w_br_a=w_br_a.astype(BF16), w_br_b=w_br_b.astype(BF16), w_out=w_out.astype(BF16),
        peer_wq=peer_wq.astype(BF16), k1=peer_keys1.astype(BF16), k2=peer_keys2.astype(BF16),
        u=peer_u.astype(BF16), vt=peer_v.T.astype(BF16),
    )


def _trunk(x, ada, positions, wp, g_mix, attend, ssm):
    gn, s, d = x.shape
    t = gn * s
    sh1, sc1, gt1, sh2, sc2, gt2 = ada
    h = _modnorm(x, g_mix.reshape(1, -1), sc1, sh1).reshape(t, d)
    cos, sin = _rope_tables(positions)
    q = _qproj(h, wp["wq"], cos, sin, wp["gq_n"], wp["gq_p"], gn)
    ckv, kpe = _latent_proj(h, wp["w_lat"], cos, sin, wp["g_ckv"], gn)
    z = _mm(h, wp["w_z"], name="proj_z")
    xbc_raw = _mm(h, wp["w_xbc"], name="proj_xbc")
    dt_raw = _mm(h, wp["w_dt"], name="proj_dt")
    gates = _mm(h, wp["w_gates"], name="proj_gates")
    k, v = _keys(ckv, kpe, wp["wuk"], wp["wuv"], wp["gk_n"], wp["gk_p"], gn)
    o_a = attend(q, k, v, ckv)
    y, new_conv, new_ssm = ssm(xbc_raw, dt_raw)
    x1, h2 = _merge(y, z, o_a, gates, x, gt1, sc2, sh2, wp["g_ssm"], wp["g_ffn"],
                    wp["w_br_a"], wp["w_br_b"], wp["w_out"])
    e1, n, e2, r2 = _peer_select(h2.reshape(t, d), wp["peer_wq"], wp["k1"], wp["k2"])
    out = _peer_mix(h2.reshape(t, d), wp["u"], wp["vt"], e1, n, e2, r2, x1, gt2)
    return out, ckv, kpe[:, :D_ROPE], new_conv, new_ssm


def kernel(x_prompt, x_sample, c_prompt, c_sample, cache_ckv, cache_kpe, page_table, state_conv, state_ssm, w_ada, b_ada, g_mix, w_in, g_q, g_ckv, w_uk, w_uv, g_k, conv_w, conv_b, dt_bias, a_log, d_skip, g_ssm, w_br_a, w_br_b, w_out, g_ffn, peer_wq, peer_keys1, peer_keys2, peer_u, peer_v):
    depth = w_ada.shape[0]
    assert depth == 1, "single-layer step"
    b, s, d = x_prompt.shape
    db, ts, _ = x_sample.shape
    assert ts == 1, "one new token per decode sequence"
    past = page_table.shape[1] * cache_ckv.shape[2]
    l = 0
    wp = _prep_weights(w_in[l], g_q[l], g_ckv[l], w_uk[l], w_uv[l], g_k[l], conv_w[l], conv_b[l], dt_bias[l],
                       a_log[l], d_skip[l], g_ssm[l], w_br_a[l], w_br_b[l], w_out[l], g_ffn[l], peer_wq[l],
                       peer_keys1[l], peer_keys2[l], peer_u[l], peer_v[l])

    ada = _ada(jnp.concatenate([c_prompt, c_sample], axis=0), w_ada[l], b_ada[l].reshape(1, -1))
    ada_p = [a.reshape(b, 1, d) for a in jnp.split(ada[:b], N_ADA, axis=-1)]
    ada_s = [a.reshape(1, db, d) for a in jnp.split(ada[b:], N_ADA, axis=-1)]

    def attend_p(q, k, v, ckv):
        return _flash(q, k, v)

    def ssm_p(xbc_raw, dt_raw):
        y, new_ssm = _ssd_prompt(xbc_raw, dt_raw, wp["conv_w"], wp["conv_b"], wp["dt_bias"], wp["a_log"],
                                 wp["d_skip_x"], b)
        new_conv = xbc_raw.reshape(b, s, CONV_DIM)[:, s - (CONV_W - 1):, :]
        return y, new_conv, new_ssm

    yp, ckv_p, kpe_p, conv_p, ssm_p_out = _trunk(x_prompt, ada_p, jnp.arange(s, dtype=I32), wp, g_mix[l],
                                                 attend_p, ssm_p)

    def attend_s(q, k, v, ckv):
        qt, qpe = _qabsorb(q, wp["wukt"], wp["gk_n"], wp["gk_p"])
        qt16 = jnp.pad(qt.transpose(1, 0, 2), ((0, 0), (0, QT_ROWS - MLA_HEADS), (0, 0)))
        olat = _decode_attn(page_table, qt16, qpe.transpose(1, 0, 2),
                            q[0].transpose(1, 0, 2), k[0].transpose(1, 0, 2), ckv.reshape(db, 1, D_LATENT),
                            wp["wukt"], cache_ckv[l], cache_kpe[l].transpose(0, 2, 1))
        return _head_out(olat.transpose(1, 0, 2), wp["wuv"])

    def ssm_s(xbc_raw, dt_raw):
        sc = state_conv[l]
        p, hh = SSM_HEADDIM, SSM_HEADS
        to_ph = lambda arr: arr.reshape(arr.shape[:-1] + (hh, p)).swapaxes(-1, -2)
        y_t, new_ssm = _ssm_step(
            to_ph(xbc_raw[:, :D_INNER]), to_ph(sc[:, :, :D_INNER]),
            to_ph(wp["conv_w"][:, :D_INNER]), to_ph(wp["conv_b"][:, :D_INNER])[0],
            xbc_raw[:, D_INNER:].reshape(db, 1, -1), sc[:, :, D_INNER:],
            wp["conv_w"][:, D_INNER:], wp["conv_b"][:, D_INNER:],
            dt_raw.reshape(db, 1, LANES), wp["dt_bias"], wp["a_log"], wp["d_skip"], state_ssm[l])
        y = y_t.swapaxes(1, 2).reshape(db, D_INNER)
        new_conv = jnp.concatenate([sc[:, 1:, :], xbc_raw[:, None, :]], axis=1)
        return y, new_conv, new_ssm

    pos_s = jnp.full((db,), past, dtype=I32)
    ys, ckv_s, kpe_s, conv_s, ssm_s_out = _trunk(x_sample.reshape(1, db, d), ada_s, pos_s, wp, g_mix[l],
                                                 attend_s, ssm_s)

    return (yp, ys.reshape(db, 1, d),
            ckv_p.reshape(1, b, s, D_LATENT), kpe_p.reshape(1, b, s, D_ROPE),
            conv_p[None], ssm_p_out[None],
            ckv_s.reshape(1, db, 1, D_LATENT), kpe_s.reshape(1, db, 1, D_ROPE),
            conv_s[None], ssm_s_out[None])
```
